```python
import math
import jax, jax.numpy as jnp
from jax import lax
import numpy as np

D_MODEL = 2048
BATCH = 2
SEQ = 16384
DEPTH = 2

CHUNK = 64
D_MIX = D_MODEL
DIFF_WIDTH = D_MIX // 4
DIFF_HEADS = 4
DIFF_HEAD_DIM = DIFF_WIDTH // (2 * DIFF_HEADS)
DIFF_V_DIM = 2 * DIFF_HEAD_DIM
LRU_WIDTH = D_MIX // 2
LRU_BLOCKS = 8
LRU_BLOCK = LRU_WIDTH // LRU_BLOCKS
CONV_WIDTH = 4
LRU_C = 8.0
SWA_WIDTH = D_MIX // 4
SWA_HEADS = 8
SWA_KV_HEADS = 2
SWA_HEAD_DIM = SWA_WIDTH // SWA_HEADS
SWA_GROUP = SWA_HEADS // SWA_KV_HEADS
WINDOW = 128
WIN_CHUNKS = WINDOW // CHUNK
D_FF = 5632
Q_BLOCK = 128
NORM_EPS = 1e-6
NEG_INF = -1e30
IN_SIZES = (2 * DIFF_HEADS * DIFF_HEAD_DIM, 2 * DIFF_HEADS * DIFF_HEAD_DIM, DIFF_HEADS * DIFF_V_DIM,
            LRU_WIDTH, LRU_WIDTH,
            SWA_HEADS * SWA_HEAD_DIM, SWA_KV_HEADS * SWA_HEAD_DIM, SWA_KV_HEADS * SWA_HEAD_DIM)
IN_COLS = sum(IN_SIZES)

kernel_name = "hybrid_diffattn_rglru_swa_macaron"


def rmsnorm(x, g):
    xf = x.astype(jnp.float32)
    y = xf * lax.rsqrt(jnp.mean(xf * xf, axis=-1, keepdims=True) + NORM_EPS)
    return (y * g.astype(jnp.float32)).astype(x.dtype)


def swiglu(h, w_gate, w_up, w_down):
    return (jax.nn.silu(h @ w_gate) * (h @ w_up)) @ w_down


def alibi_slopes(n):
    return 2.0 ** (-8.0 * jnp.arange(1, n + 1, dtype=jnp.float32) / n)


def diff_attention(q, k, v, lam, lam_init, subln_g):
    b, s = q.shape[:2]
    nqb = s // Q_BLOCK
    scale = DIFF_HEAD_DIM ** -0.5
    slopes = alibi_slopes(DIFF_HEADS)
    kpos = jnp.arange(s)
    kchunk = kpos // CHUNK
    qb = q.reshape(b, nqb, Q_BLOCK, DIFF_HEADS, 2, DIFF_HEAD_DIM).transpose(1, 0, 2, 3, 4, 5)

    def block(args):
        qi, i = args
        qpos = i * Q_BLOCK + jnp.arange(Q_BLOCK)
        dist = jnp.abs(qpos[:, None] - kpos[None, :]).astype(jnp.float32)
        allowed = kchunk[None, :] <= (qpos // CHUNK)[:, None]
        bias = jnp.where(allowed[None], -slopes[:, None, None] * dist[None], NEG_INF)
        logits = jnp.einsum('bqhmd,bkhmd->bmhqk', qi, k,
                            preferred_element_type=jnp.float32) * scale + bias[None, None]
        p = jax.nn.softmax(logits, axis=-1)
        w = p[:, 0] - lam * p[:, 1]
        return jnp.einsum('bhqk,bkhd->bqhd', w.astype(v.dtype), v)

    o = lax.map(block, (qb, jnp.arange(nqb)))
    o = o.transpose(1, 0, 2, 3, 4).reshape(b, s, DIFF_HEADS, DIFF_V_DIM)
    o = rmsnorm(o, subln_g) * (1.0 - lam_init)
    return o.reshape(b, s, DIFF_WIDTH)


def rg_lru_branch(xr, gate, conv_w, conv_b, w_r, b_r, w_i, b_i, lam):
    b, s, _ = xr.shape
    xc = lax.conv_general_dilated(xr, conv_w[:, None, :].astype(xr.dtype), window_strides=(1,),
                                  padding=[(CONV_WIDTH - 1, 0)],
                                  dimension_numbers=('NWC', 'WIO', 'NWC'),
                                  feature_group_count=LRU_WIDTH) + conv_b
    xblk = xc.reshape(b, s, LRU_BLOCKS, LRU_BLOCK)
    r = jax.nn.sigmoid(jnp.einsum('bsnc,ncd->bsnd', xblk, w_r).reshape(b, s, LRU_WIDTH) + b_r)
    i = jax.nn.sigmoid(jnp.einsum('bsnc,ncd->bsnd', xblk, w_i).reshape(b, s, LRU_WIDTH) + b_i)
    log_a = -LRU_C * r.astype(jnp.float32) * jax.nn.softplus(-lam.astype(jnp.float32))
    a = jnp.exp(log_a)
    u = jnp.sqrt(-jnp.expm1(2.0 * log_a)) * (i * xc).astype(jnp.float32)

    def combine(left, right):
        a1, b1 = left
        a2, b2 = right
        return a1 * a2, a2 * b1 + b2

    _, h = lax.associative_scan(combine, (a, u), axis=1)
    return jax.nn.gelu(gate) * h.astype(gate.dtype)


def swa_sink_attention(q, k, v, sinks):
    b, s = q.shape[:2]
    nc = s // CHUNK
    span = (WIN_CHUNKS + 1) * CHUNK
    scale = SWA_HEAD_DIM ** -0.5
    qc = q.reshape(b, nc, CHUNK, SWA_KV_HEADS, SWA_GROUP, SWA_HEAD_DIM)

    def band(t):
        tp = jnp.pad(t, ((0, 0), (WIN_CHUNKS * CHUNK, 0), (0, 0), (0, 0)))
        tp = tp.reshape(b, nc + WIN_CHUNKS, CHUNK, SWA_KV_HEADS, SWA_HEAD_DIM)
        return jnp.concatenate([tp[:, j:j + nc] for j in range(WIN_CHUNKS + 1)], axis=2)

    kb, vb = band(k), band(v)
    qi = jnp.arange(CHUNK)
    kj = jnp.arange(span)
    dist = jnp.abs(qi[:, None] + WIN_CHUNKS * CHUNK - kj[None, :]).astype(jnp.float32)
    slopes = alibi_slopes(SWA_HEADS).reshape(SWA_KV_HEADS, SWA_GROUP)
    bias = -slopes[:, :, None, None] * dist
    valid = (jnp.arange(nc)[:, None] * CHUNK - WIN_CHUNKS * CHUNK + kj[None, :]) >= 0
    logits = jnp.einsum('bcqkgd,bcskd->bckgqs', qc, kb,
                        preferred_element_type=jnp.float32) * scale + bias[None, None]
    logits = jnp.where(valid[None, :, None, None, None, :], logits, NEG_INF)
    sink = jnp.broadcast_to(sinks.astype(jnp.float32).reshape(SWA_KV_HEADS, SWA_GROUP)[None, None, :, :, None, None],
                            logits.shape[:-1] + (1,))
    p = jax.nn.softmax(jnp.concatenate([logits, sink], axis=-1), axis=-1)[..., :span]
    o = jnp.einsum('bckgqs,bcskd->bcqkgd', p.astype(v.dtype), vb)
    return o.reshape(b, s, SWA_WIDTH)


def setup_inputs(seed: int = 0) -> dict:
    key = jax.random.key(seed)
    ks = iter(jax.random.split(key, 40))

    def nrm(shape, scale):
        return jax.random.normal(next(ks), shape, jnp.float32) * scale

    def gain(shape):
        return 1.0 + nrm(shape, 0.02)

    a0 = jax.random.uniform(next(ks), (DEPTH, LRU_WIDTH), jnp.float32, 0.9, 0.999) ** (1.0 / LRU_C)
    return {
        "x": nrm((BATCH, SEQ, D_MODEL), 1.0),
        "ffn1_norm": gain((DEPTH, D_MODEL)),
        "ffn1_w_gate": nrm((DEPTH, D_MODEL, D_FF), D_MODEL ** -0.5),
        "ffn1_w_up": nrm((DEPTH, D_MODEL, D_FF), D_MODEL ** -0.5),
        "ffn1_w_down": nrm((DEPTH, D_FF, D_MODEL), D_FF ** -0.5),
        "mix_norm": gain((DEPTH, D_MODEL)),
        "w_in": nrm((DEPTH, D_MODEL, IN_COLS), D_MODEL ** -0.5),
        "diff_lq1": nrm((DEPTH, DIFF_HEAD_DIM), 0.1),
        "diff_lk1": nrm((DEPTH, DIFF_HEAD_DIM), 0.1),
        "diff_lq2": nrm((DEPTH, DIFF_HEAD_DIM), 0.1),
        "diff_lk2": nrm((DEPTH, DIFF_HEAD_DIM), 0.1),
        "diff_subln": gain((DEPTH, DIFF_V_DIM)),
        "lru_conv_w": nrm((DEPTH, CONV_WIDTH, LRU_WIDTH), CONV_WIDTH ** -0.5),
        "lru_conv_b": nrm((DEPTH, LRU_WIDTH), 0.01),
        "lru_w_rgate": nrm((DEPTH, LRU_BLOCKS, LRU_BLOCK, LRU_BLOCK), LRU_BLOCK ** -0.5),
        "lru_b_rgate": nrm((DEPTH, LRU_WIDTH), 0.01),
        "lru_w_igate": nrm((DEPTH, LRU_BLOCKS, LRU_BLOCK, LRU_BLOCK), LRU_BLOCK ** -0.5),
        "lru_b_igate": nrm((DEPTH, LRU_WIDTH), 0.01),
        "lru_lambda": jnp.log(a0) - jnp.log1p(-a0),
        "lru_out_norm": gain((DEPTH, LRU_WIDTH)),
        "swa_sinks": nrm((DEPTH, SWA_HEADS), 0.5),
        "swa_out_norm": gain((DEPTH, SWA_WIDTH)),
        "w_out": nrm((DEPTH, D_MIX, D_MODEL), D_MIX ** -0.5),
        "ffn2_norm": gain((DEPTH, D_MODEL)),
        "ffn2_w_gate": nrm((DEPTH, D_MODEL, D_FF), D_MODEL ** -0.5),
        "ffn2_w_up": nrm((DEPTH, D_MODEL, D_FF), D_MODEL ** -0.5),
        "ffn2_w_down": nrm((DEPTH, D_FF, D_MODEL), D_FF ** -0.5),
        "final_norm": gain((D_MODEL,)),
    }


def reference(x, ffn1_norm, ffn1_w_gate, ffn1_w_up, ffn1_w_down, mix_norm, w_in,
              diff_lq1, diff_lk1, diff_lq2, diff_lk2, diff_subln,
              lru_conv_w, lru_conv_b, lru_w_rgate, lru_b_rgate, lru_w_igate, lru_b_igate,
              lru_lambda, lru_out_norm, swa_sinks, swa_out_norm, w_out,
              ffn2_norm, ffn2_w_gate, ffn2_w_up, ffn2_w_down, final_norm):
    b, s, _ = x.shape
    splits = [int(v) for v in np.cumsum(IN_SIZES)[:-1]]
    for l in range(DEPTH):
        x = x + 0.5 * swiglu(rmsnorm(x, ffn1_norm[l]), ffn1_w_gate[l], ffn1_w_up[l], ffn1_w_down[l])

        h = rmsnorm(x, mix_norm[l])
        proj = h @ w_in[l]
        a_q, a_k, a_v, b_x, b_g, c_q, c_k, c_v = jnp.split(proj, splits, axis=-1)

        lam_init = 0.8 - 0.6 * math.exp(-0.3 * l)
        lam = (jnp.exp(jnp.sum(diff_lq1[l] * diff_lk1[l]).astype(jnp.float32))
               - jnp.exp(jnp.sum(diff_lq2[l] * diff_lk2[l]).astype(jnp.float32)) + lam_init)
        out_a = diff_attention(a_q.reshape(b, s, DIFF_HEADS, 2, DIFF_HEAD_DIM),
                               a_k.reshape(b, s, DIFF_HEADS, 2, DIFF_HEAD_DIM),
                               a_v.reshape(b, s, DIFF_HEADS, DIFF_V_DIM),
                               lam, lam_init, diff_subln[l])

        out_b = rg_lru_branch(b_x, b_g, lru_conv_w[l], lru_conv_b[l], lru_w_rgate[l], lru_b_rgate[l],
                              lru_w_igate[l], lru_b_igate[l], lru_lambda[l])
        out_b = rmsnorm(out_b, lru_out_norm[l])

        out_c = swa_sink_attention(c_q.reshape(b, s, SWA_HEADS, SWA_HEAD_DIM),
                                   c_k.reshape(b, s, SWA_KV_HEADS, SWA_HEAD_DIM),
                                   c_v.reshape(b, s, SWA_KV_HEADS, SWA_HEAD_DIM),
                                   swa_sinks[l])
        out_c = rmsnorm(out_c, swa_out_norm[l])

        x = x + jnp.concatenate([out_a, out_b, out_c], axis=-1) @ w_out[l]

        x = x + 0.5 * swiglu(rmsnorm(x, ffn2_norm[l]), ffn2_w_gate[l], ffn2_w_up[l], ffn2_w_down[l])
    return rmsnorm(x, final_norm)
```

```python
import functools
import math

import jax
import jax.numpy as jnp
from jax import lax
from jax.experimental import pallas as pl
from jax.experimental.pallas import tpu as pltpu

F32 = jnp.float32
BF16 = jnp.bfloat16

CHUNK = 64
DIFF_HEADS = 4
DIFF_HEAD_DIM = 64
DIFF_V_DIM = 128
DIFF_WIDTH = DIFF_HEADS * DIFF_V_DIM
LRU_WIDTH = 1024
LRU_BLOCKS = 8
LRU_BLOCK = 128
CONV_WIDTH = 4
LRU_C = 8.0
SWA_HEADS = 8
SWA_KV_HEADS = 2
SWA_HEAD_DIM = 64
SWA_GROUP = SWA_HEADS // SWA_KV_HEADS
SWA_WIDTH = SWA_HEADS * SWA_HEAD_DIM
SWA_KV_WIDTH = SWA_KV_HEADS * SWA_HEAD_DIM
WINDOW = 128
NORM_EPS = 1e-6
NEG_INF = -1e30

COL_AQ = 0
COL_BX = 3 * DIFF_WIDTH
COL_BG = COL_BX + LRU_WIDTH
COL_CQ = COL_BG + LRU_WIDTH
IN_COLS = COL_CQ + SWA_WIDTH + 2 * SWA_KV_WIDTH
A_COLS = 3 * DIFF_WIDTH
C_COLS = SWA_WIDTH + 2 * SWA_KV_WIDTH

VMEM_LIMIT_BYTES = 56 * 1024 * 1024

FFN_TM = 1024
FFN_TF = 256
FFN_SUB = 256
MIX_TM = 512
ATT_T = 512
SWA_TQ = 256


def _rms(x, g):
    return x * lax.rsqrt(jnp.mean(x * x, axis=-1, keepdims=True) + NORM_EPS) * g


def _chunk_of(pos):
    return lax.shift_right_logical(pos, jnp.int32(CHUNK.bit_length() - 1))


def _cparams(sem):
    return pltpu.CompilerParams(dimension_semantics=sem, vmem_limit_bytes=VMEM_LIMIT_BYTES)


def _ffn_body(*refs, n_f, final):
    if final:
        x_ref, g_ref, wg_ref, wu_ref, wd_ref, fg_ref, o_ref, hn_ref = refs
    else:
        x_ref, g_ref, wg_ref, wu_ref, wd_ref, o_ref, hn_ref = refs
    f = pl.program_id(1)
    tm = x_ref.shape[0]

    @pl.when(f == 0)
    def _():
        for r in range(tm // FFN_SUB):
            rows = pl.ds(r * FFN_SUB, FFN_SUB)
            hn_ref[rows, :] = _rms(x_ref[rows, :], g_ref[...]).astype(BF16)
            o_ref[rows, :] = jnp.zeros((FFN_SUB, o_ref.shape[1]), F32)

    for r in range(tm // FFN_SUB):
        rows = pl.ds(r * FFN_SUB, FFN_SUB)
        h = hn_ref[rows, :]
        g = jnp.dot(h, wg_ref[...], preferred_element_type=F32)
        u = jnp.dot(h, wu_ref[...], preferred_element_type=F32)
        mid = (g * jax.nn.sigmoid(g) * u).astype(BF16)
        o_ref[rows, :] += jnp.dot(mid, wd_ref[...], preferred_element_type=F32)

    @pl.when(f == n_f - 1)
    def _():
        for r in range(tm // FFN_SUB):
            rows = pl.ds(r * FFN_SUB, FFN_SUB)
            y = x_ref[rows, :] + 0.5 * o_ref[rows, :]
            if final:
                y = _rms(y, fg_ref[...])
            o_ref[rows, :] = y


def _ffn(x, norm_g, w_gate, w_up, w_down, final_g=None):
    n, d = x.shape
    d_ff = w_gate.shape[1]
    tm = min(FFN_TM, n)
    n_f = d_ff // FFN_TF
    final = final_g is not None
    in_specs = [
        pl.BlockSpec((tm, d), lambda i, f: (i, 0)),
        pl.BlockSpec((1, d), lambda i, f: (0, 0)),
        pl.BlockSpec((d, FFN_TF), lambda i, f: (0, f)),
        pl.BlockSpec((d, FFN_TF), lambda i, f: (0, f)),
        pl.BlockSpec((FFN_TF, d), lambda i, f: (f, 0)),
    ]
    args = [x, norm_g.reshape(1, d), w_gate, w_up, w_down]
    if final:
        in_specs.append(pl.BlockSpec((1, d), lambda i, f: (0, 0)))
        args.append(final_g.reshape(1, d))
    return pl.pallas_call(
        functools.partial(_ffn_body, n_f=n_f, final=final),
        grid=(n // tm, n_f),
        in_specs=in_specs,
        out_specs=pl.BlockSpec((tm, d), lambda i, f: (i, 0)),
        out_shape=jax.ShapeDtypeStruct((n, d), F32),
        scratch_shapes=[pltpu.VMEM((tm, d), BF16)],
        compiler_params=_cparams(("parallel", "arbitrary")),
        name="ffn_final" if final else "ffn",
    )(*args)


def _mix_in_body(x_ref, g_ref, w_ref, cw_ref, cb_ref, wr_ref, br_ref, wi_ref, bi_ref, lam_ref, on_ref,
                 aqkv_ref, cqkv_ref, ob_ref, hn_ref, bx_ref, xc_ref, a_ref, u_ref, hc_ref, *, tiles_per_seq):
    tm = x_ref.shape[0]
    pad = 8

    @pl.when(pl.program_id(0) % tiles_per_seq == 0)
    def _():
        bx_ref[0:pad, :] = jnp.zeros((pad, LRU_WIDTH), F32)
        hc_ref[...] = jnp.zeros_like(hc_ref)

    hn_ref[...] = _rms(x_ref[...], g_ref[...]).astype(BF16)

    def proj(c0, width):
        return jnp.dot(hn_ref[...], w_ref[:, c0:c0 + width], preferred_element_type=F32)

    aqkv_ref[:, 0:DIFF_WIDTH] = (proj(COL_AQ, DIFF_WIDTH) * (DIFF_HEAD_DIM ** -0.5)).astype(BF16)
    aqkv_ref[:, DIFF_WIDTH:2 * DIFF_WIDTH] = proj(COL_AQ + DIFF_WIDTH, DIFF_WIDTH).astype(BF16)
    aqkv_ref[:, 2 * DIFF_WIDTH:A_COLS] = proj(COL_AQ + 2 * DIFF_WIDTH, DIFF_WIDTH).astype(BF16)
    cqkv_ref[:, 0:SWA_WIDTH] = (proj(COL_CQ, SWA_WIDTH) * (SWA_HEAD_DIM ** -0.5)).astype(BF16)
    cqkv_ref[:, SWA_WIDTH:C_COLS] = proj(COL_CQ + SWA_WIDTH, 2 * SWA_KV_WIDTH).astype(BF16)

    for c in range(LRU_WIDTH // 512):
        bx_ref[pad:pad + tm, c * 512:(c + 1) * 512] = proj(COL_BX + c * 512, 512)
    xc = cb_ref[...] + cw_ref[0:1, :] * bx_ref[pad - 3:pad - 3 + tm, :]
    for j in range(1, CONV_WIDTH):
        xc = xc + cw_ref[j:j + 1, :] * bx_ref[pad - 3 + j:pad - 3 + j + tm, :]
    xc_ref[...] = xc
    bx_ref[pad - 3:pad, :] = bx_ref[pad - 3 + tm:pad + tm, :]

    neg_c_sp = -LRU_C * jax.nn.softplus(-lam_ref[...])
    for nb in range(LRU_BLOCKS):
        sl = slice(nb * LRU_BLOCK, (nb + 1) * LRU_BLOCK)
        xcs = xc_ref[:, sl]
        xb = xcs.astype(BF16)
        r = jax.nn.sigmoid(jnp.dot(xb, wr_ref[nb], preferred_element_type=F32) + br_ref[:, sl])
        ig = jax.nn.sigmoid(jnp.dot(xb, wi_ref[nb], preferred_element_type=F32) + bi_ref[:, sl])
        log_a = r * neg_c_sp[:, sl]
        a_ref[:, sl] = jnp.exp(log_a)
        th = jnp.tanh(log_a)
        u_ref[:, sl] = jnp.sqrt(-2.0 * th / (1.0 - th)) * (ig * xcs)

    def step(t, h):
        row = pl.ds(t, 1)
        h = a_ref[row, :] * h + u_ref[row, :]
        u_ref[row, :] = h
        return h

    hc_ref[...] = lax.fori_loop(0, tm, step, hc_ref[...], unroll=8)

    for c in range(LRU_WIDTH // 512):
        cs = slice(c * 512, (c + 1) * 512)
        a_ref[:, cs] = jax.nn.gelu(proj(COL_BG + c * 512, 512)) * u_ref[:, cs]
    ob_ref[...] = _rms(a_ref[...], on_ref[...]).astype(BF16)


def _mix_in(x, norm_g, w_in, conv_w, conv_b, w_r, b_r, w_i, b_i, lam, out_norm, seq):
    n, d = x.shape
    tm = min(MIX_TM, seq)
    row = lambda v: v.reshape(1, -1)
    const2 = lambda i: (0, 0)
    const3 = lambda i: (0, 0, 0)
    wvec = pl.BlockSpec((1, LRU_WIDTH), const2)
    gate_w = pl.BlockSpec((LRU_BLOCKS, LRU_BLOCK, LRU_BLOCK), const3)
    return pl.pallas_call(
        functools.partial(_mix_in_body, tiles_per_seq=seq // tm),
        grid=(n // tm,),
        in_specs=[
            pl.BlockSpec((tm, d), lambda i: (i, 0)),
            pl.BlockSpec((1, d), const2),
            pl.BlockSpec((d, IN_COLS), const2, pipeline_mode=pl.Buffered(1)),
            pl.BlockSpec((CONV_WIDTH, LRU_WIDTH), const2),
            wvec, gate_w, wvec, gate_w, wvec, wvec, wvec,
        ],
        out_specs=[
            pl.BlockSpec((tm, A_COLS), lambda i: (i, 0)),
            pl.BlockSpec((tm, C_COLS), lambda i: (i, 0)),
            pl.BlockSpec((tm, LRU_WIDTH), lambda i: (i, 0)),
        ],
        out_shape=[
            jax.ShapeDtypeStruct((n, A_COLS), BF16),
            jax.ShapeDtypeStruct((n, C_COLS), BF16),
            jax.ShapeDtypeStruct((n, LRU_WIDTH), BF16),
        ],
        scratch_shapes=[
            pltpu.VMEM((tm, d), BF16),
            pltpu.VMEM((tm + 8, LRU_WIDTH), F32),
            pltpu.VMEM((tm, LRU_WIDTH), F32),
            pltpu.VMEM((tm, LRU_WIDTH), F32),
            pltpu.VMEM((tm, LRU_WIDTH), F32),
            pltpu.VMEM((1, LRU_WIDTH), F32),
        ],
        compiler_params=_cparams(("arbitrary",)),
        name="mix_in",
    )(x, row(norm_g), w_in, conv_w, row(conv_b), w_r, row(b_r), w_i, row(b_i), row(lam), row(out_norm))


def _diff_body(slopes_ref, q_ref, k_ref, v_ref, lq1_ref, lk1_ref, lq2_ref, lk2_ref, sg_ref,
               o_ref, m_ref, l_ref, acc_ref, *, lam_init):
    t = q_ref.shape[0]
    h = pl.program_id(1)
    qi = pl.program_id(2)
    slope = slopes_ref[h]

    q = q_ref[...]
    lane = lax.broadcasted_iota(jnp.int32, q.shape, 1)
    zero = jnp.zeros_like(q)
    qq = jnp.concatenate([jnp.where(lane < DIFF_HEAD_DIM, q, zero),
                          jnp.where(lane >= DIFF_HEAD_DIM, q, zero)], axis=0)

    m_ref[...] = jnp.full_like(m_ref, NEG_INF)
    l_ref[...] = jnp.zeros_like(l_ref)
    acc_ref[...] = jnp.zeros_like(acc_ref)

    def update(s, v):
        m_prev = m_ref[...]
        m_new = jnp.maximum(m_prev, jnp.max(s, axis=1, keepdims=True))
        alpha = jnp.exp(m_prev - m_new)
        p = jnp.exp(s - m_new)
        l_ref[...] = alpha * l_ref[...] + jnp.sum(p, axis=1, keepdims=True)
        acc_ref[...] = alpha * acc_ref[...] + jnp.dot(p.astype(BF16), v, preferred_element_type=F32)
        m_ref[...] = m_new

    def scores(kt):
        rows = pl.ds(pl.multiple_of(kt * t, t), t)
        s = lax.dot_general(qq, k_ref[rows, :], (((1,), (1,)), ((), ())), preferred_element_type=F32)
        return s, v_ref[rows, :]

    def past_tile(kt, carry):
        s, v = scores(kt)
        rel = (kt - qi) * t + lax.broadcasted_iota(jnp.int32, (1, t), 1)
        update(s + slope * rel.astype(F32), v)
        return carry

    lax.fori_loop(0, qi, past_tile, 0)

    s, v = scores(qi)
    ii = lax.broadcasted_iota(jnp.int32, (t, t), 0)
    jj = lax.broadcasted_iota(jnp.int32, (t, t), 1)
    bias = jnp.where(_chunk_of(jj) <= _chunk_of(ii), slope * (ii - jnp.abs(ii - jj)).astype(F32), NEG_INF)
    allowed = jnp.concatenate([bias, bias], axis=0)
    update(jnp.where(allowed > 0.5 * NEG_INF, s + allowed, NEG_INF), v)

    lam = (jnp.exp(jnp.sum(lq1_ref[...] * lk1_ref[...], axis=-1, keepdims=True))
           - jnp.exp(jnp.sum(lq2_ref[...] * lk2_ref[...], axis=-1, keepdims=True)) + lam_init)
    o = acc_ref[0:t, :] / l_ref[0:t, :] - lam * (acc_ref[t:2 * t, :] / l_ref[t:2 * t, :])
    o_ref[...] = (_rms(o, sg_ref[...]) * (1.0 - lam_init)).astype(BF16)


def _diff_attn(aqkv, lq1, lk1, lq2, lk2, subln_g, lam_init, batch, seq):
    n = aqkv.shape[0]
    t = min(ATT_T, seq)
    nq = seq // t
    slopes = 2.0 ** (-8.0 * jnp.arange(1, DIFF_HEADS + 1, dtype=F32) / DIFF_HEADS)
    row = lambda v: v.reshape(1, -1)
    vec = lambda w: pl.BlockSpec((1, w), lambda b, h, i: (0, 0))
    return pl.pallas_call(
        functools.partial(_diff_body, lam_init=lam_init),
        grid=(batch, DIFF_HEADS, nq),
        in_specs=[
            pl.BlockSpec(memory_space=pltpu.SMEM),
            pl.BlockSpec((t, DIFF_V_DIM), lambda b, h, i: (b * nq + i, h)),
            pl.BlockSpec((seq, DIFF_V_DIM), lambda b, h, i: (b, DIFF_HEADS + h)),
            pl.BlockSpec((seq, DIFF_V_DIM), lambda b, h, i: (b, 2 * DIFF_HEADS + h)),
            vec(DIFF_HEAD_DIM), vec(DIFF_HEAD_DIM), vec(DIFF_HEAD_DIM), vec(DIFF_HEAD_DIM),
            vec(DIFF_V_DIM),
        ],
        out_specs=pl.BlockSpec((t, DIFF_V_DIM), lambda b, h, i: (b * nq + i, h)),
        out_shape=jax.ShapeDtypeStruct((n, DIFF_WIDTH), BF16),
        scratch_shapes=[
            pltpu.VMEM((2 * t, 1), F32),
            pltpu.VMEM((2 * t, 1), F32),
            pltpu.VMEM((2 * t, DIFF_V_DIM), F32),
        ],
        compiler_params=_cparams(("parallel", "parallel", "arbitrary")),
        name="diff_attn",
    )(slopes, aqkv, aqkv, aqkv, row(lq1), row(lk1), row(lq2), row(lk2), row(subln_g))


def _swa_body(slopes_ref, sinks_ref, q_ref, kvp_ref, kvm_ref, on_ref, o_ref, bias_ref):
    tq = q_ref.shape[0]
    span = WINDOW + tq
    ti = pl.program_id(1)

    @pl.when((pl.program_id(0) == 0) & (ti == 0))
    def _():
        ii = lax.broadcasted_iota(jnp.int32, (tq, span), 0)
        jj = lax.broadcasted_iota(jnp.int32, (tq, span), 1)
        kc = _chunk_of(jj) - WINDOW // CHUNK
        qc = _chunk_of(ii)
        band = (kc <= qc) & (kc >= qc - WINDOW // CHUNK)
        dist = jnp.abs(ii + WINDOW - jj).astype(F32)
        for hh in range(SWA_HEADS):
            bias_ref[hh] = jnp.where(band, -slopes_ref[hh] * dist, NEG_INF)

    jmin = jnp.where(ti == 0, WINDOW, 0)
    in_seq = lax.broadcasted_iota(jnp.int32, (tq, span), 1) >= jmin

    outs = []
    for kvh in range(SWA_KV_HEADS):
        ks = slice(kvh * SWA_HEAD_DIM, (kvh + 1) * SWA_HEAD_DIM)
        vs = slice(SWA_KV_WIDTH + kvh * SWA_HEAD_DIM, SWA_KV_WIDTH + (kvh + 1) * SWA_HEAD_DIM)
        k = jnp.concatenate([kvp_ref[:, ks], kvm_ref[:, ks]], axis=0)
        v = jnp.concatenate([kvp_ref[:, vs], kvm_ref[:, vs]], axis=0)
        for g in range(SWA_GROUP):
            hh = kvh * SWA_GROUP + g
            qh = q_ref[:, hh * SWA_HEAD_DIM:(hh + 1) * SWA_HEAD_DIM]
            s = lax.dot_general(qh, k, (((1,), (1,)), ((), ())), preferred_element_type=F32)
            s = jnp.where(in_seq, s + bias_ref[hh], NEG_INF)
            sink = sinks_ref[hh]
            m = jnp.maximum(jnp.max(s, axis=1, keepdims=True), sink)
            p = jnp.exp(s - m)
            denom = jnp.sum(p, axis=1, keepdims=True) + jnp.exp(sink - m)
            outs.append(jnp.dot(p.astype(BF16), v, preferred_element_type=F32) / denom)
    o_ref[...] = _rms(jnp.concatenate(outs, axis=1), on_ref[...]).astype(BF16)


def _swa(cqkv, sinks, out_norm, batch, seq):
    n = cqkv.shape[0]
    tq = min(SWA_TQ, seq)
    nq = seq // tq
    wpt = tq // WINDOW
    kv_col = SWA_WIDTH // (2 * SWA_KV_WIDTH)
    slopes = 2.0 ** (-8.0 * jnp.arange(1, SWA_HEADS + 1, dtype=F32) / SWA_HEADS)
    return pl.pallas_call(
        _swa_body,
        grid=(batch, nq),
        in_specs=[
            pl.BlockSpec(memory_space=pltpu.SMEM),
            pl.BlockSpec(memory_space=pltpu.SMEM),
            pl.BlockSpec((tq, SWA_WIDTH), lambda b, i: (b * nq + i, 0)),
            pl.BlockSpec((WINDOW, 2 * SWA_KV_WIDTH),
                         lambda b, i: ((b * nq + i) * wpt - jnp.where(i == 0, 0, 1), kv_col)),
            pl.BlockSpec((tq, 2 * SWA_KV_WIDTH), lambda b, i: (b * nq + i, kv_col)),
            pl.BlockSpec((1, SWA_WIDTH), lambda b, i: (0, 0)),
        ],
        out_specs=pl.BlockSpec((tq, SWA_WIDTH), lambda b, i: (b * nq + i, 0)),
        out_shape=jax.ShapeDtypeStruct((n, SWA_WIDTH), BF16),
        scratch_shapes=[pltpu.VMEM((SWA_HEADS, tq, WINDOW + tq), F32)],
        compiler_params=_cparams(("arbitrary", "arbitrary")),
        name="swa",
    )(slopes, sinks.astype(F32), cqkv, cqkv, cqkv, out_norm.reshape(1, -1))


def _out_proj_body(x_ref, oa_ref, ob_ref, oc_ref, w_ref, o_ref):
    y = jnp.dot(oa_ref[...], w_ref[0:DIFF_WIDTH, :], preferred_element_type=F32)
    y = y + jnp.dot(ob_ref[...], w_ref[DIFF_WIDTH:DIFF_WIDTH + LRU_WIDTH, :], preferred_element_type=F32)
    y = y + jnp.dot(oc_ref[...], w_ref[DIFF_WIDTH + LRU_WIDTH:, :], preferred_element_type=F32)
    o_ref[...] = x_ref[...] + y


def _out_proj(x, oa, ob, oc, w_out):
    n, d = x.shape
    tm = min(MIX_TM, n)
    tok = lambda w: pl.BlockSpec((tm, w), lambda i: (i, 0))
    return pl.pallas_call(
        _out_proj_body,
        grid=(n // tm,),
        in_specs=[tok(d), tok(DIFF_WIDTH), tok(LRU_WIDTH), tok(SWA_WIDTH),
                  pl.BlockSpec(w_out.shape, lambda i: (0, 0))],
        out_specs=tok(d),
        out_shape=jax.ShapeDtypeStruct((n, d), F32),
        compiler_params=_cparams(("parallel",)),
        name="out_proj",
    )(x, oa, ob, oc, w_out)


def kernel(x, ffn1_norm, ffn1_w_gate, ffn1_w_up, ffn1_w_down, mix_norm, w_in, diff_lq1, diff_lk1, diff_lq2,
           diff_lk2, diff_subln, lru_conv_w, lru_conv_b, lru_w_rgate, lru_b_rgate, lru_w_igate, lru_b_igate,
           lru_lambda, lru_out_norm, swa_sinks, swa_out_norm, w_out, ffn2_norm, ffn2_w_gate, ffn2_w_up,
           ffn2_w_down, final_norm):
    batch, seq, d = x.shape
    depth = w_in.shape[0]
    h = x.reshape(batch * seq, d)
    bf = lambda w: w.astype(BF16)
    for l in range(depth):
        h = _ffn(h, ffn1_norm[l], bf(ffn1_w_gate[l]), bf(ffn1_w_up[l]), bf(ffn1_w_down[l]))
        aqkv, cqkv, ob = _mix_in(h, mix_norm[l], bf(w_in[l]), lru_conv_w[l], lru_conv_b[l],
                                 bf(lru_w_rgate[l]), lru_b_rgate[l], bf(lru_w_igate[l]), lru_b_igate[l],
                                 lru_lambda[l], lru_out_norm[l], seq)
        lam_init = 0.8 - 0.6 * math.exp(-0.3 * l)
        oa = _diff_attn(aqkv, diff_lq1[l], diff_lk1[l], diff_lq2[l], diff_lk2[l], diff_subln[l],
                        lam_init, batch, seq)
        oc = _swa(cqkv, swa_sinks[l], swa_out_norm[l], batch, seq)
        h = _out_proj(h, oa, ob, oc, bf(w_out[l]))
        h = _ffn(h, ffn2_norm[l], bf(ffn2_w_gate[l]), bf(ffn2_w_up[l]), bf(ffn2_w_down[l]),
                 final_g=final_norm if l == depth - 1 else None)
    return h.reshape(batch, seq, d)
```

```python
import functools
import math

import jax
import jax.numpy as jnp
from jax import lax
from jax.experimental import pallas as pl
from jax.experimental.pallas import tpu as pltpu

F32 = jnp.float32
BF16 = jnp.bfloat16

CHUNK = 64
DIFF_HEADS = 4
DIFF_HEAD_DIM = 64
DIFF_V_DIM = 128
DIFF_WIDTH = DIFF_HEADS * DIFF_V_DIM
LRU_WIDTH = 1024
LRU_BLOCKS = 8
LRU_BLOCK = 128
CONV_WIDTH = 4
LRU_C = 8.0
SWA_HEADS = 8
SWA_KV_HEADS = 2
SWA_HEAD_DIM = 64
SWA_GROUP = SWA_HEADS // SWA_KV_HEADS
SWA_WIDTH = SWA_HEADS * SWA_HEAD_DIM
SWA_KV_WIDTH = SWA_KV_HEADS * SWA_HEAD_DIM
WINDOW = 128
NORM_EPS = 1e-6
NEG_INF = -1e30
LOG2E = math.log2(math.e)
ONES_ROWS = 16

COL_AQ = 0
COL_BX = 3 * DIFF_WIDTH
COL_BG = COL_BX + LRU_WIDTH
COL_CQ = COL_BG + LRU_WIDTH
IN_COLS = COL_CQ + SWA_WIDTH + 2 * SWA_KV_WIDTH
A_COLS = 3 * DIFF_WIDTH
C_COLS = SWA_WIDTH + 2 * SWA_KV_WIDTH

VMEM_LIMIT_BYTES = 60 * 1024 * 1024

FFN_TM = 1024
FFN_TF = 512
FFN_SUB = 256
MIX_TM = 512
ATT_T = 512
SWA_TQ = 256


def _rms(x, g):
    return x * lax.rsqrt(jnp.mean(x * x, axis=-1, keepdims=True) + NORM_EPS) * g


def _chunk_of(pos):
    return lax.shift_right_logical(pos, jnp.int32(CHUNK.bit_length() - 1))


def _cparams(sem):
    return pltpu.CompilerParams(dimension_semantics=sem, vmem_limit_bytes=VMEM_LIMIT_BYTES)


def _ffn_body(*refs, n_f, final):
    if final:
        x_ref, g_ref, wg_ref, wu_ref, wd_ref, fg_ref, o_ref, hn_ref = refs
    else:
        x_ref, g_ref, wg_ref, wu_ref, wd_ref, o_ref, hn_ref = refs
    f = pl.program_id(1)
    tm = x_ref.shape[0]

    @pl.when(f == 0)
    def _():
        for r in range(tm // FFN_SUB):
            rows = pl.ds(r * FFN_SUB, FFN_SUB)
            x = x_ref[rows, :]
            hn_ref[rows, :] = _rms(x, g_ref[...]).astype(BF16)
            o_ref[rows, :] = x

    for r in range(tm // FFN_SUB):
        rows = pl.ds(r * FFN_SUB, FFN_SUB)
        h = hn_ref[rows, :]
        g = jnp.dot(h, wg_ref[...], preferred_element_type=F32)
        u = jnp.dot(h, wu_ref[...], preferred_element_type=F32)
        mid = (0.5 * g * jax.nn.sigmoid(g) * u).astype(BF16)
        o_ref[rows, :] += jnp.dot(mid, wd_ref[...], preferred_element_type=F32)

    if final:
        @pl.when(f == n_f - 1)
        def _():
            for r in range(tm // FFN_SUB):
                rows = pl.ds(r * FFN_SUB, FFN_SUB)
                o_ref[rows, :] = _rms(o_ref[rows, :], fg_ref[...])


def _ffn(x, norm_g, w_gate, w_up, w_down, final_g=None):
    n, d = x.shape
    d_ff = w_gate.shape[1]
    tm = min(FFN_TM, n)
    n_f = d_ff // FFN_TF
    final = final_g is not None
    in_specs = [
        pl.BlockSpec((tm, d), lambda i, f: (i, 0)),
        pl.BlockSpec((1, d), lambda i, f: (0, 0)),
        pl.BlockSpec((d, FFN_TF), lambda i, f: (0, f)),
        pl.BlockSpec((d, FFN_TF), lambda i, f: (0, f)),
        pl.BlockSpec((FFN_TF, d), lambda i, f: (f, 0)),
    ]
    args = [x, norm_g.reshape(1, d), w_gate, w_up, w_down]
    if final:
        in_specs.append(pl.BlockSpec((1, d), lambda i, f: (0, 0)))
        args.append(final_g.reshape(1, d))
    return pl.pallas_call(
        functools.partial(_ffn_body, n_f=n_f, final=final),
        grid=(n // tm, n_f),
        in_specs=in_specs,
        out_specs=pl.BlockSpec((tm, d), lambda i, f: (i, 0)),
        out_shape=jax.ShapeDtypeStruct((n, d), F32),
        scratch_shapes=[pltpu.VMEM((tm, d), BF16)],
        compiler_params=_cparams(("parallel", "arbitrary")),
        name="ffn_final" if final else "ffn",
    )(*args)


def _mix_in_body(x_ref, g_ref, w_ref, cw_ref, cb_ref, wr_ref, br_ref, wi_ref, bi_ref, lam_ref, on_ref,
                 aqkv_ref, cqkv_ref, ob_ref, hn_ref, bx_ref, xc_ref, a_ref, u_ref, hc_ref, *, tiles_per_seq):
    tm = x_ref.shape[0]
    pad = 8

    @pl.when(pl.program_id(0) % tiles_per_seq == 0)
    def _():
        bx_ref[0:pad, :] = jnp.zeros((pad, LRU_WIDTH), F32)
        hc_ref[...] = jnp.zeros_like(hc_ref)

    hn_ref[...] = _rms(x_ref[...], g_ref[...]).astype(BF16)

    def proj(c0, width):
        return jnp.dot(hn_ref[...], w_ref[:, c0:c0 + width], preferred_element_type=F32)

    aqkv_ref[:, 0:DIFF_WIDTH] = (proj(COL_AQ, DIFF_WIDTH) * (LOG2E * DIFF_HEAD_DIM ** -0.5)).astype(BF16)
    aqkv_ref[:, DIFF_WIDTH:2 * DIFF_WIDTH] = proj(COL_AQ + DIFF_WIDTH, DIFF_WIDTH).astype(BF16)
    aqkv_ref[:, 2 * DIFF_WIDTH:A_COLS] = proj(COL_AQ + 2 * DIFF_WIDTH, DIFF_WIDTH).astype(BF16)
    cqkv_ref[:, 0:SWA_WIDTH] = (proj(COL_CQ, SWA_WIDTH) * (SWA_HEAD_DIM ** -0.5)).astype(BF16)
    cqkv_ref[:, SWA_WIDTH:C_COLS] = proj(COL_CQ + SWA_WIDTH, 2 * SWA_KV_WIDTH).astype(BF16)

    for c in range(LRU_WIDTH // 512):
        bx_ref[pad:pad + tm, c * 512:(c + 1) * 512] = proj(COL_BX + c * 512, 512)
    xc = cb_ref[...] + cw_ref[0:1, :] * bx_ref[pad - 3:pad - 3 + tm, :]
    for j in range(1, CONV_WIDTH):
        xc = xc + cw_ref[j:j + 1, :] * bx_ref[pad - 3 + j:pad - 3 + j + tm, :]
    xc_ref[...] = xc
    bx_ref[pad - 3:pad, :] = bx_ref[pad - 3 + tm:pad + tm, :]

    neg_c_sp = -LRU_C * jax.nn.softplus(-lam_ref[...])
    for nb in range(LRU_BLOCKS):
        sl = slice(nb * LRU_BLOCK, (nb + 1) * LRU_BLOCK)
        xcs = xc_ref[:, sl]
        xb = xcs.astype(BF16)
        r = jax.nn.sigmoid(jnp.dot(xb, wr_ref[nb], preferred_element_type=F32) + br_ref[:, sl])
        ig = jax.nn.sigmoid(jnp.dot(xb, wi_ref[nb], preferred_element_type=F32) + bi_ref[:, sl])
        log_a = r * neg_c_sp[:, sl]
        a_ref[:, sl] = jnp.exp(log_a)
        th = jnp.tanh(log_a)
        u_ref[:, sl] = jnp.sqrt(-2.0 * th / (1.0 - th)) * (ig * xcs)

    def step(t, h):
        row = pl.ds(t, 1)
        h = a_ref[row, :] * h + u_ref[row, :]
        u_ref[row, :] = h
        return h

    hc_ref[...] = lax.fori_loop(0, tm, step, hc_ref[...], unroll=8)

    for c in range(LRU_WIDTH // 512):
        cs = slice(c * 512, (c + 1) * 512)
        a_ref[:, cs] = jax.nn.gelu(proj(COL_BG + c * 512, 512)) * u_ref[:, cs]
    ob_ref[...] = _rms(a_ref[...], on_ref[...]).astype(BF16)


def _mix_in(x, norm_g, w_in, conv_w, conv_b, w_r, b_r, w_i, b_i, lam, out_norm, seq):
    n, d = x.shape
    tm = min(MIX_TM, seq)
    row = lambda v: v.reshape(1, -1)
    const2 = lambda i: (0, 0)
    const3 = lambda i: (0, 0, 0)
    wvec = pl.BlockSpec((1, LRU_WIDTH), const2)
    gate_w = pl.BlockSpec((LRU_BLOCKS, LRU_BLOCK, LRU_BLOCK), const3)
    return pl.pallas_call(
        functools.partial(_mix_in_body, tiles_per_seq=seq // tm),
        grid=(n // tm,),
        in_specs=[
            pl.BlockSpec((tm, d), lambda i: (i, 0)),
            pl.BlockSpec((1, d), const2),
            pl.BlockSpec((d, IN_COLS), const2, pipeline_mode=pl.Buffered(1)),
            pl.BlockSpec((CONV_WIDTH, LRU_WIDTH), const2),
            wvec, gate_w, wvec, gate_w, wvec, wvec, wvec,
        ],
        out_specs=[
            pl.BlockSpec((tm, A_COLS), lambda i: (i, 0)),
            pl.BlockSpec((tm, C_COLS), lambda i: (i, 0)),
            pl.BlockSpec((tm, LRU_WIDTH), lambda i: (i, 0)),
        ],
        out_shape=[
            jax.ShapeDtypeStruct((n, A_COLS), BF16),
            jax.ShapeDtypeStruct((n, C_COLS), BF16),
            jax.ShapeDtypeStruct((n, LRU_WIDTH), BF16),
        ],
        scratch_shapes=[
            pltpu.VMEM((tm, d), BF16),
            pltpu.VMEM((tm + 8, LRU_WIDTH), F32),
            pltpu.VMEM((tm, LRU_WIDTH), F32),
            pltpu.VMEM((tm, LRU_WIDTH), F32),
            pltpu.VMEM((tm, LRU_WIDTH), F32),
            pltpu.VMEM((1, LRU_WIDTH), F32),
        ],
        compiler_params=_cparams(("arbitrary",)),
        name="mix_in",
    )(x, row(norm_g), w_in, conv_w, row(conv_b), w_r, row(b_r), w_i, row(b_i), row(lam), row(out_norm))


def _diff_body(slopes_ref, q_ref, k_ref, v_ref, lq1_ref, lk1_ref, lq2_ref, lk2_ref, sg_ref, o_ref,
               vt_ref, kb_ref, z0_ref, z1_ref, mm0_ref, mm1_ref, al0_ref, al1_ref, m_ref, acc_ref,
               *, lam_init):
    t = q_ref.shape[0]
    r = 2 * t
    dv = v_ref.shape[1]
    h = pl.program_id(1)
    qi = pl.program_id(2)
    slope = slopes_ref[h] * LOG2E
    z_refs, mm_refs, al_refs = (z0_ref, z1_ref), (mm0_ref, mm1_ref), (al0_ref, al1_ref)

    @pl.when(qi == 0)
    def _():
        def xpose(c, carry):
            rows = pl.ds(pl.multiple_of(c * t, t), t)
            vt_ref[c, 0:dv, :] = v_ref[rows, :].T
            vt_ref[c, dv:dv + ONES_ROWS, :] = jnp.ones((ONES_ROWS, t), BF16)
            return carry

        lax.fori_loop(0, k_ref.shape[0] // t, xpose, 0)
        kb_ref[...] = slope * lax.broadcasted_iota(jnp.int32, kb_ref.shape, 0).astype(F32)

    q = q_ref[...]
    lane = lax.broadcasted_iota(jnp.int32, q.shape, 1)
    zero = jnp.zeros_like(q)
    qq = jnp.concatenate([jnp.where(lane < DIFF_HEAD_DIM, q, zero),
                          jnp.where(lane >= DIFF_HEAD_DIM, q, zero)], axis=0)

    m_ref[...] = jnp.full_like(m_ref, NEG_INF)
    acc_ref[...] = jnp.zeros_like(acc_ref)

    def pass1(kt, slot, diag):
        rows = pl.ds(pl.multiple_of(kt * t, t), t)
        s = lax.dot_general(k_ref[rows, :], qq, (((1,), (1,)), ((), ())), preferred_element_type=F32)
        if diag:
            jj = lax.broadcasted_iota(jnp.int32, (t, r), 0)
            ii = jnp.bitwise_and(lax.broadcasted_iota(jnp.int32, (t, r), 1), t - 1)
            z = jnp.where(_chunk_of(jj) <= _chunk_of(ii), s + slope * (ii - jnp.abs(ii - jj)).astype(F32), NEG_INF)
            shift = 0.0
        else:
            z = s + pltpu.repeat(kb_ref[...], r // kb_ref.shape[1], axis=1)
            shift = slope * ((kt - qi) * t).astype(F32)
        z_refs[slot][...] = z
        m_prev = m_ref[...]
        m_new = jnp.maximum(m_prev, jnp.max(z, axis=0, keepdims=True) + shift)
        m_ref[...] = m_new
        mm_refs[slot][...] = m_new - shift
        al_refs[slot][...] = jnp.exp2(m_prev - m_new)

    def pass2(tile, slot):
        p = jnp.exp2(z_refs[slot][...] - mm_refs[slot][...]).astype(BF16)
        acc_ref[...] = al_refs[slot][...] * acc_ref[...] + jnp.dot(vt_ref[tile], p, preferred_element_type=F32)

    pass1(qi, 0, True)

    def two_tiles(jj, carry):
        j = 2 * jj
        pass1(j, 1, False)
        pass2(jnp.where(j == 0, qi, j - 1), 0)
        pass1(j + 1, 0, False)
        pass2(j, 1)
        return carry

    lax.fori_loop(0, qi // 2, two_tiles, 0)

    @pl.when(qi % 2 == 1)
    def _():
        pass1(qi - 1, 1, False)
        pass2(jnp.where(qi == 1, qi, qi - 2), 0)
        pass2(qi - 1, 1)

    @pl.when(qi % 2 == 0)
    def _():
        pass2(jnp.where(qi == 0, qi, qi - 1), 0)

    lam = (jnp.exp(jnp.sum(lq1_ref[...] * lk1_ref[...], axis=-1, keepdims=True))
           - jnp.exp(jnp.sum(lq2_ref[...] * lk2_ref[...], axis=-1, keepdims=True)) + lam_init)
    inv = 1.0 / acc_ref[dv:dv + 1, :]
    o_t = acc_ref[0:dv, 0:t] * inv[:, 0:t] - lam * (acc_ref[0:dv, t:r] * inv[:, t:r])
    o_ref[...] = (_rms(o_t.T, sg_ref[...]) * (1.0 - lam_init)).astype(BF16)


def _diff_attn(aqkv, lq1, lk1, lq2, lk2, subln_g, lam_init, batch, seq):
    n = aqkv.shape[0]
    t = min(ATT_T, seq)
    assert t & (t - 1) == 0 and seq % t == 0 and t % CHUNK == 0
    nq = seq // t
    slopes = 2.0 ** (-8.0 * jnp.arange(1, DIFF_HEADS + 1, dtype=F32) / DIFF_HEADS)
    row = lambda v: v.reshape(1, -1)
    vec = lambda w: pl.BlockSpec((1, w), lambda b, h, i: (0, 0))
    stat = pltpu.VMEM((1, 2 * t), F32)
    return pl.pallas_call(
        functools.partial(_diff_body, lam_init=lam_init),
        grid=(batch, DIFF_HEADS, nq),
        in_specs=[
            pl.BlockSpec(memory_space=pltpu.SMEM),
            pl.BlockSpec((t, DIFF_V_DIM), lambda b, h, i: (b * nq + i, h)),
            pl.BlockSpec((seq, DIFF_V_DIM), lambda b, h, i: (b, DIFF_HEADS + h)),
            pl.BlockSpec((seq, DIFF_V_DIM), lambda b, h, i: (b, 2 * DIFF_HEADS + h)),
            vec(DIFF_HEAD_DIM), vec(DIFF_HEAD_DIM), vec(DIFF_HEAD_DIM), vec(DIFF_HEAD_DIM),
            vec(DIFF_V_DIM),
        ],
        out_specs=pl.BlockSpec((t, DIFF_V_DIM), lambda b, h, i: (b * nq + i, h)),
        out_shape=jax.ShapeDtypeStruct((n, DIFF_WIDTH), BF16),
        scratch_shapes=[
            pltpu.VMEM((nq, DIFF_V_DIM + ONES_ROWS, t), BF16),
            pltpu.VMEM((t, 128), F32),
            pltpu.VMEM((t, 2 * t), F32),
            pltpu.VMEM((t, 2 * t), F32),
            stat, stat,
            stat, stat,
            stat,
            pltpu.VMEM((DIFF_V_DIM + ONES_ROWS, 2 * t), F32),
        ],
        compiler_params=_cparams(("parallel", "parallel", "arbitrary")),
        name="diff_attn",
    )(slopes, aqkv, aqkv, aqkv, row(lq1), row(lk1), row(lq2), row(lk2), row(subln_g))


def _swa_body(slopes_ref, sinks_ref, q_ref, kvp_ref, kvm_ref, on_ref, o_ref, bias_ref):
    tq = q_ref.shape[0]
    span = WINDOW + tq
    ti = pl.program_id(1)

    @pl.when((pl.program_id(0) == 0) & (ti == 0))
    def _():
        ii = lax.broadcasted_iota(jnp.int32, (tq, span), 0)
        jj = lax.broadcasted_iota(jnp.int32, (tq, span), 1)
        kc = _chunk_of(jj) - WINDOW // CHUNK
        qc = _chunk_of(ii)
        band = (kc <= qc) & (kc >= qc - WINDOW // CHUNK)
        dist = jnp.abs(ii + WINDOW - jj).astype(F32)
        for hh in range(SWA_HEADS):
            bias_ref[hh] = jnp.where(band, -slopes_ref[hh] * dist, NEG_INF)

    jmin = jnp.where(ti == 0, WINDOW, 0)
    in_seq = lax.broadcasted_iota(jnp.int32, (tq, span), 1) >= jmin

    outs = []
    for kvh in range(SWA_KV_HEADS):
        ks = slice(kvh * SWA_HEAD_DIM, (kvh + 1) * SWA_HEAD_DIM)
        vs = slice(SWA_KV_WIDTH + kvh * SWA_HEAD_DIM, SWA_KV_WIDTH + (kvh + 1) * SWA_HEAD_DIM)
        k = jnp.concatenate([kvp_ref[:, ks], kvm_ref[:, ks]], axis=0)
        v = jnp.concatenate([kvp_ref[:, vs], kvm_ref[:, vs]], axis=0)
        for g in range(SWA_GROUP):
            hh = kvh * SWA_GROUP + g
            qh = q_ref[:, hh * SWA_HEAD_DIM:(hh + 1) * SWA_HEAD_DIM]
            s = lax.dot_general(qh, k, (((1,), (1,)), ((), ())), preferred_element_type=F32)
            s = jnp.where(in_seq, s + bias_ref[hh], NEG_INF)
            sink = sinks_ref[hh]
            m = jnp.maximum(jnp.max(s, axis=1, keepdims=True), sink)
            p = jnp.exp(s - m)
            denom = jnp.sum(p, axis=1, keepdims=True) + jnp.exp(sink - m)
            outs.append(jnp.dot(p.astype(BF16), v, preferred_element_type=F32) / denom)
    o_ref[...] = _rms(jnp.concatenate(outs, axis=1), on_ref[...]).astype(BF16)


def _swa(cqkv, sinks, out_norm, batch, seq):
    n = cqkv.shape[0]
    tq = min(SWA_TQ, seq)
    nq = seq // tq
    wpt = tq // WINDOW
    kv_col = SWA_WIDTH // (2 * SWA_KV_WIDTH)
    slopes = 2.0 ** (-8.0 * jnp.arange(1, SWA_HEADS + 1, dtype=F32) / SWA_HEADS)
    return pl.pallas_call(
        _swa_body,
        grid=(batch, nq),
        in_specs=[
            pl.BlockSpec(memory_space=pltpu.SMEM),
            pl.BlockSpec(memory_space=pltpu.SMEM),
            pl.BlockSpec((tq, SWA_WIDTH), lambda b, i: (b * nq + i, 0)),
            pl.BlockSpec((WINDOW, 2 * SWA_KV_WIDTH),
                         lambda b, i: ((b * nq + i) * wpt - jnp.where(i == 0, 0, 1), kv_col)),
            pl.BlockSpec((tq, 2 * SWA_KV_WIDTH), lambda b, i: (b * nq + i, kv_col)),
            pl.BlockSpec((1, SWA_WIDTH), lambda b, i: (0, 0)),
        ],
        out_specs=pl.BlockSpec((tq, SWA_WIDTH), lambda b, i: (b * nq + i, 0)),
        out_shape=jax.ShapeDtypeStruct((n, SWA_WIDTH), BF16),
        scratch_shapes=[pltpu.VMEM((SWA_HEADS, tq, WINDOW + tq), F32)],
        compiler_params=_cparams(("arbitrary", "arbitrary")),
        name="swa",
    )(slopes, sinks.astype(F32), cqkv, cqkv, cqkv, out_norm.reshape(1, -1))


def _out_proj_body(x_ref, oa_ref, ob_ref, oc_ref, w_ref, o_ref):
    y = jnp.dot(oa_ref[...], w_ref[0:DIFF_WIDTH, :], preferred_element_type=F32)
    y = y + jnp.dot(ob_ref[...], w_ref[DIFF_WIDTH:DIFF_WIDTH + LRU_WIDTH, :], preferred_element_type=F32)
    y = y + jnp.dot(oc_ref[...], w_ref[DIFF_WIDTH + LRU_WIDTH:, :], preferred_element_type=F32)
    o_ref[...] = x_ref[...] + y


def _out_proj(x, oa, ob, oc, w_out):
    n, d = x.shape
    tm = min(MIX_TM, n)
    tok = lambda w: pl.BlockSpec((tm, w), lambda i: (i, 0))
    return pl.pallas_call(
        _out_proj_body,
        grid=(n // tm,),
        in_specs=[tok(d), tok(DIFF_WIDTH), tok(LRU_WIDTH), tok(SWA_WIDTH),
                  pl.BlockSpec(w_out.shape, lambda i: (0, 0))],
        out_specs=tok(d),
        out_shape=jax.ShapeDtypeStruct((n, d), F32),
        compiler_params=_cparams(("parallel",)),
        name="out_proj",
    )(x, oa, ob, oc, w_out)


def kernel(x, ffn1_norm, ffn1_w_gate, ffn1_w_up, ffn1_w_down, mix_norm, w_in, diff_lq1, diff_lk1, diff_lq2,
           diff_lk2, diff_subln, lru_conv_w, lru_conv_b, lru_w_rgate, lru_b_rgate, lru_w_igate, lru_b_igate,
           lru_lambda, lru_out_norm, swa_sinks, swa_out_norm, w_out, ffn2_norm, ffn2_w_gate, ffn2_w_up,
           ffn2_w_down, final_norm):
    batch, seq, d = x.shape
    depth = w_in.shape[0]
    h = x.reshape(batch * seq, d)
    bf = lambda w: w.astype(BF16)
    for l in range(depth):
        h = _ffn(h, ffn1_norm[l], bf(ffn1_w_gate[l]), bf(ffn1_w_up[l]), bf(ffn1_w_down[l]))
        aqkv, cqkv, ob = _mix_in(h, mix_norm[l], bf(w_in[l]), lru_conv_w[l], lru_conv_b[l],
                                 bf(lru_w_rgate[l]), lru_b_rgate[l], bf(lru_w_igate[l]), lru_b_igate[l],
                                 lru_lambda[l], lru_out_norm[l], seq)
        lam_init = 0.8 - 0.6 * math.exp(-0.3 * l)
        oa = _diff_attn(aqkv, diff_lq1[l], diff_lk1[l], diff_lq2[l], diff_lk2[l], diff_subln[l],
                        lam_init, batch, seq)
        oc = _swa(cqkv, swa_sinks[l], swa_out_norm[l], batch, seq)
        h = _out_proj(h, oa, ob, oc, bf(w_out[l]))
        h = _ffn(h, ffn2_norm[l], bf(ffn2_w_gate[l]), bf(ffn2_w_up[l]), bf(ffn2_w_down[l]),
                 final_g=final_norm if l == depth - 1 else None)
    return h.reshape(batch, seq, d)
```

```python
import functools
import math

import jax
import jax.numpy as jnp
from jax import lax
from jax.experimental import pallas as pl
from jax.experimental.pallas import tpu as pltpu

F32 = jnp.float32
BF16 = jnp.bfloat16

CHUNK = 64
DIFF_HEADS = 4
DIFF_HEAD_DIM = 64
DIFF_V_DIM = 128
DIFF_WIDTH = DIFF_HEADS * DIFF_V_DIM
LRU_WIDTH = 1024
LRU_BLOCKS = 8
LRU_BLOCK = 128
CONV_WIDTH = 4
LRU_C = 8.0
SWA_HEADS = 8
SWA_KV_HEADS = 2
SWA_HEAD_DIM = 64
SWA_GROUP = SWA_HEADS // SWA_KV_HEADS
SWA_WIDTH = SWA_HEADS * SWA_HEAD_DIM
SWA_KV_WIDTH = SWA_KV_HEADS * SWA_HEAD_DIM
WINDOW = 128
NORM_EPS = 1e-6
NEG_INF = -1e30
LOG2E = math.log2(math.e)
ONES_ROWS = 16

COL_AQ = 0
COL_BX = 3 * DIFF_WIDTH
COL_BG = COL_BX + LRU_WIDTH
COL_CQ = COL_BG + LRU_WIDTH
IN_COLS = COL_CQ + SWA_WIDTH + 2 * SWA_KV_WIDTH
A_COLS = 3 * DIFF_WIDTH
C_COLS = SWA_WIDTH + 2 * SWA_KV_WIDTH

VMEM_LIMIT_BYTES = 60 * 1024 * 1024

FFN_TM = 1024
FFN_TF = 512
FFN_SUB = 512
MIX_TM = 512
ATT_T = 512
ATT_UNROLL = 4
SWA_TQ = 256


def _rms(x, g):
    return x * lax.rsqrt(jnp.mean(x * x, axis=-1, keepdims=True) + NORM_EPS) * g


def _chunk_of(pos):
    return lax.shift_right_logical(pos, jnp.int32(CHUNK.bit_length() - 1))


def _cparams(sem):
    return pltpu.CompilerParams(dimension_semantics=sem, vmem_limit_bytes=VMEM_LIMIT_BYTES)


def _ffn_body(*refs, n_f, final):
    if final:
        x_ref, g_ref, wg_ref, wu_ref, wd_ref, fg_ref, o_ref, hn_ref = refs
    else:
        x_ref, g_ref, wg_ref, wu_ref, wd_ref, o_ref, hn_ref = refs
    f = pl.program_id(1)
    tm = x_ref.shape[0]

    @pl.when(f == 0)
    def _():
        for r in range(tm // FFN_SUB):
            rows = pl.ds(r * FFN_SUB, FFN_SUB)
            x = x_ref[rows, :]
            hn_ref[rows, :] = _rms(x, g_ref[...]).astype(BF16)
            o_ref[rows, :] = x

    for r in range(tm // FFN_SUB):
        rows = pl.ds(r * FFN_SUB, FFN_SUB)
        h = hn_ref[rows, :]
        g = jnp.dot(h, wg_ref[...], preferred_element_type=F32)
        u = jnp.dot(h, wu_ref[...], preferred_element_type=F32)
        mid = (0.5 * g * jax.nn.sigmoid(g) * u).astype(BF16)
        o_ref[rows, :] += jnp.dot(mid, wd_ref[...], preferred_element_type=F32)

    if final:
        @pl.when(f == n_f - 1)
        def _():
            for r in range(tm // FFN_SUB):
                rows = pl.ds(r * FFN_SUB, FFN_SUB)
                o_ref[rows, :] = _rms(o_ref[rows, :], fg_ref[...])


def _ffn(x, norm_g, w_gate, w_up, w_down, final_g=None):
    n, d = x.shape
    d_ff = w_gate.shape[1]
    tm = min(FFN_TM, n)
    n_f = d_ff // FFN_TF
    final = final_g is not None
    in_specs = [
        pl.BlockSpec((tm, d), lambda i, f: (i, 0)),
        pl.BlockSpec((1, d), lambda i, f: (0, 0)),
        pl.BlockSpec((d, FFN_TF), lambda i, f: (0, f)),
        pl.BlockSpec((d, FFN_TF), lambda i, f: (0, f)),
        pl.BlockSpec((FFN_TF, d), lambda i, f: (f, 0)),
    ]
    args = [x, norm_g.reshape(1, d), w_gate, w_up, w_down]
    if final:
        in_specs.append(pl.BlockSpec((1, d), lambda i, f: (0, 0)))
        args.append(final_g.reshape(1, d))
    return pl.pallas_call(
        functools.partial(_ffn_body, n_f=n_f, final=final),
        grid=(n // tm, n_f),
        in_specs=in_specs,
        out_specs=pl.BlockSpec((tm, d), lambda i, f: (i, 0)),
        out_shape=jax.ShapeDtypeStruct((n, d), F32),
        scratch_shapes=[pltpu.VMEM((tm, d), BF16)],
        compiler_params=_cparams(("parallel", "arbitrary")),
        name="ffn_final" if final else "ffn",
    )(*args)


def _mix_in_body(x_ref, g_ref, w_ref, cw_ref, cb_ref, wr_ref, br_ref, wi_ref, bi_ref, lam_ref, on_ref,
                 aqkv_ref, cqkv_ref, ob_ref, hn_ref, bx_ref, xc_ref, a_ref, u_ref, gate_ref, hc_ref, *, tiles_per_seq):
    tm = x_ref.shape[0]
    pad = 8

    @pl.when(pl.program_id(0) % tiles_per_seq == 0)
    def _():
        bx_ref[0:pad, :] = jnp.zeros((pad, LRU_WIDTH), F32)
        hc_ref[...] = jnp.zeros_like(hc_ref)

    hn_ref[...] = _rms(x_ref[...], g_ref[...]).astype(BF16)

    def proj(c0, width):
        return jnp.dot(hn_ref[...], w_ref[:, c0:c0 + width], preferred_element_type=F32)

    for c in range(LRU_WIDTH // 512):
        bx_ref[pad:pad + tm, c * 512:(c + 1) * 512] = proj(COL_BX + c * 512, 512)
    xc = cb_ref[...] + cw_ref[0:1, :] * bx_ref[pad - 3:pad - 3 + tm, :]
    for j in range(1, CONV_WIDTH):
        xc = xc + cw_ref[j:j + 1, :] * bx_ref[pad - 3 + j:pad - 3 + j + tm, :]
    xc_ref[...] = xc
    bx_ref[pad - 3:pad, :] = bx_ref[pad - 3 + tm:pad + tm, :]

    neg_c_sp = -LRU_C * jax.nn.softplus(-lam_ref[...])
    for nb in range(LRU_BLOCKS):
        sl = slice(nb * LRU_BLOCK, (nb + 1) * LRU_BLOCK)
        xcs = xc_ref[:, sl]
        xb = xcs.astype(BF16)
        r = jax.nn.sigmoid(jnp.dot(xb, wr_ref[nb], preferred_element_type=F32) + br_ref[:, sl])
        ig = jax.nn.sigmoid(jnp.dot(xb, wi_ref[nb], preferred_element_type=F32) + bi_ref[:, sl])
        log_a = r * neg_c_sp[:, sl]
        a_ref[:, sl] = jnp.exp(log_a)
        th = jnp.tanh(log_a)
        u_ref[:, sl] = jnp.sqrt(-2.0 * th / (1.0 - th)) * (ig * xcs)

    h = hc_ref[...]
    for t in range(tm):
        h = a_ref[t:t + 1, :] * h + u_ref[t:t + 1, :]
        u_ref[t:t + 1, :] = h
    hc_ref[...] = h

    aqkv_ref[:, 0:DIFF_WIDTH] = (proj(COL_AQ, DIFF_WIDTH) * (LOG2E * DIFF_HEAD_DIM ** -0.5)).astype(BF16)
    aqkv_ref[:, DIFF_WIDTH:2 * DIFF_WIDTH] = proj(COL_AQ + DIFF_WIDTH, DIFF_WIDTH).astype(BF16)
    aqkv_ref[:, 2 * DIFF_WIDTH:A_COLS] = proj(COL_AQ + 2 * DIFF_WIDTH, DIFF_WIDTH).astype(BF16)
    cqkv_ref[:, 0:SWA_WIDTH] = (proj(COL_CQ, SWA_WIDTH) * (SWA_HEAD_DIM ** -0.5)).astype(BF16)
    cqkv_ref[:, SWA_WIDTH:C_COLS] = proj(COL_CQ + SWA_WIDTH, 2 * SWA_KV_WIDTH).astype(BF16)
    for c in range(LRU_WIDTH // 512):
        gate_ref[:, c * 512:(c + 1) * 512] = jax.nn.gelu(proj(COL_BG + c * 512, 512))

    ob_ref[...] = _rms(gate_ref[...] * u_ref[...], on_ref[...]).astype(BF16)


def _mix_in(x, norm_g, w_in, conv_w, conv_b, w_r, b_r, w_i, b_i, lam, out_norm, seq):
    n, d = x.shape
    tm = min(MIX_TM, seq)
    row = lambda v: v.reshape(1, -1)
    const2 = lambda i: (0, 0)
    const3 = lambda i: (0, 0, 0)
    wvec = pl.BlockSpec((1, LRU_WIDTH), const2)
    gate_w = pl.BlockSpec((LRU_BLOCKS, LRU_BLOCK, LRU_BLOCK), const3)
    return pl.pallas_call(
        functools.partial(_mix_in_body, tiles_per_seq=seq // tm),
        grid=(n // tm,),
        in_specs=[
            pl.BlockSpec((tm, d), lambda i: (i, 0)),
            pl.BlockSpec((1, d), const2),
            pl.BlockSpec((d, IN_COLS), const2, pipeline_mode=pl.Buffered(1)),
            pl.BlockSpec((CONV_WIDTH, LRU_WIDTH), const2),
            wvec, gate_w, wvec, gate_w, wvec, wvec, wvec,
        ],
        out_specs=[
            pl.BlockSpec((tm, A_COLS), lambda i: (i, 0)),
            pl.BlockSpec((tm, C_COLS), lambda i: (i, 0)),
            pl.BlockSpec((tm, LRU_WIDTH), lambda i: (i, 0)),
        ],
        out_shape=[
            jax.ShapeDtypeStruct((n, A_COLS), BF16),
            jax.ShapeDtypeStruct((n, C_COLS), BF16),
            jax.ShapeDtypeStruct((n, LRU_WIDTH), BF16),
        ],
        scratch_shapes=[
            pltpu.VMEM((tm, d), BF16),
            pltpu.VMEM((tm + 8, LRU_WIDTH), F32),
            pltpu.VMEM((tm, LRU_WIDTH), F32),
            pltpu.VMEM((tm, LRU_WIDTH), F32),
            pltpu.VMEM((tm, LRU_WIDTH), F32),
            pltpu.VMEM((tm, LRU_WIDTH), F32),
            pltpu.VMEM((1, LRU_WIDTH), F32),
        ],
        compiler_params=_cparams(("arbitrary",)),
        name="mix_in",
    )(x, row(norm_g), w_in, conv_w, row(conv_b), w_r, row(b_r), w_i, row(b_i), row(lam), row(out_norm))


def _diff_body(slopes_ref, q_ref, k_ref, v_ref, lq1_ref, lk1_ref, lq2_ref, lk2_ref, sg_ref, o_ref,
               vt_ref, kb_ref, db_ref, z0_ref, z1_ref, mm0_ref, mm1_ref, al0_ref, al1_ref, m_ref, acc_ref,
               *, lam_init):
    t = q_ref.shape[0]
    r = 2 * t
    dv = v_ref.shape[1]
    h = pl.program_id(1)
    qi = pl.program_id(2)
    slope = slopes_ref[h] * LOG2E
    z_refs, mm_refs, al_refs = (z0_ref, z1_ref), (mm0_ref, mm1_ref), (al0_ref, al1_ref)

    @pl.when(qi == 0)
    def _():
        def xpose(c, carry):
            rows = pl.ds(pl.multiple_of(c * t, t), t)
            vt_ref[c, 0:dv, :] = v_ref[rows, :].T
            vt_ref[c, dv:dv + ONES_ROWS, :] = jnp.ones((ONES_ROWS, t), BF16)
            return carry

        lax.fori_loop(0, k_ref.shape[0] // t, xpose, 0)
        kb_ref[...] = slope * lax.broadcasted_iota(jnp.int32, kb_ref.shape, 0).astype(F32)
        jj = lax.broadcasted_iota(jnp.int32, (t, t), 0)
        ii = lax.broadcasted_iota(jnp.int32, (t, t), 1)
        db_ref[...] = jnp.where(_chunk_of(jj) <= _chunk_of(ii), slope * (ii - jnp.abs(ii - jj)).astype(F32), NEG_INF)

    q = q_ref[...]
    lane = lax.broadcasted_iota(jnp.int32, q.shape, 1)
    zero = jnp.zeros_like(q)
    qq = jnp.concatenate([jnp.where(lane < DIFF_HEAD_DIM, q, zero),
                          jnp.where(lane >= DIFF_HEAD_DIM, q, zero)], axis=0)

    m_ref[...] = jnp.full_like(m_ref, NEG_INF)
    acc_ref[...] = jnp.zeros_like(acc_ref)

    def pass1(kt, slot, diag):
        rows = pl.ds(pl.multiple_of(kt * t, t), t)
        s = lax.dot_general(k_ref[rows, :], qq, (((1,), (1,)), ((), ())), preferred_element_type=F32)
        if diag:
            z = s + jnp.concatenate([db_ref[...]] * 2, axis=1)
            shift = 0.0
        else:
            z = s + jnp.concatenate([kb_ref[...]] * (r // kb_ref.shape[1]), axis=1)
            shift = slope * ((kt - qi) * t).astype(F32)
        z_refs[slot][...] = z
        m_prev = m_ref[...]
        m_new = jnp.maximum(m_prev, jnp.max(z, axis=0, keepdims=True) + shift)
        m_ref[...] = m_new
        mm_refs[slot][...] = m_new - shift
        al_refs[slot][...] = jnp.exp2(m_prev - m_new)

    def pass2(tile, slot):
        p = jnp.exp2(z_refs[slot][...] - mm_refs[slot][...]).astype(BF16)
        acc_ref[...] = al_refs[slot][...] * acc_ref[...] + jnp.dot(vt_ref[tile], p, preferred_element_type=F32)

    def step(j, parity):
        pass1(j, 1 - parity, False)
        pass2(jnp.where(j == 0, qi, j - 1), parity)

    pass1(qi, 0, True)

    def unrolled(jj, carry):
        for u in range(ATT_UNROLL):
            step(ATT_UNROLL * jj + u, u % 2)
        return carry

    lax.fori_loop(0, qi // ATT_UNROLL, unrolled, 0)
    done = (qi // ATT_UNROLL) * ATT_UNROLL
    for u in range(ATT_UNROLL - 1):
        @pl.when(qi - done > u)
        def _():
            step(done + u, u % 2)

    for parity in range(2):
        @pl.when(qi % 2 == parity)
        def _():
            pass2(jnp.where(qi == 0, qi, qi - 1), parity)

    lam = (jnp.exp(jnp.sum(lq1_ref[...] * lk1_ref[...], axis=-1, keepdims=True))
           - jnp.exp(jnp.sum(lq2_ref[...] * lk2_ref[...], axis=-1, keepdims=True)) + lam_init)
    inv = 1.0 / acc_ref[dv:dv + 1, :]
    o_t = acc_ref[0:dv, 0:t] * inv[:, 0:t] - lam * (acc_ref[0:dv, t:r] * inv[:, t:r])
    o_ref[...] = (_rms(o_t.T, sg_ref[...]) * (1.0 - lam_init)).astype(BF16)


def _diff_attn(aqkv, lq1, lk1, lq2, lk2, subln_g, lam_init, batch, seq):
    n = aqkv.shape[0]
    t = min(ATT_T, seq)
    assert t & (t - 1) == 0 and seq % t == 0 and t % CHUNK == 0
    nq = seq // t
    slopes = 2.0 ** (-8.0 * jnp.arange(1, DIFF_HEADS + 1, dtype=F32) / DIFF_HEADS)
    row = lambda v: v.reshape(1, -1)
    vec = lambda w: pl.BlockSpec((1, w), lambda b, h, i: (0, 0))
    stat = pltpu.VMEM((1, 2 * t), F32)
    return pl.pallas_call(
        functools.partial(_diff_body, lam_init=lam_init),
        grid=(batch, DIFF_HEADS, nq),
        in_specs=[
            pl.BlockSpec(memory_space=pltpu.SMEM),
            pl.BlockSpec((t, DIFF_V_DIM), lambda b, h, i: (b * nq + i, h)),
            pl.BlockSpec((seq, DIFF_V_DIM), lambda b, h, i: (b, DIFF_HEADS + h)),
            pl.BlockSpec((seq, DIFF_V_DIM), lambda b, h, i: (b, 2 * DIFF_HEADS + h)),
            vec(DIFF_HEAD_DIM), vec(DIFF_HEAD_DIM), vec(DIFF_HEAD_DIM), vec(DIFF_HEAD_DIM),
            vec(DIFF_V_DIM),
        ],
        out_specs=pl.BlockSpec((t, DIFF_V_DIM), lambda b, h, i: (b * nq + i, h)),
        out_shape=jax.ShapeDtypeStruct((n, DIFF_WIDTH), BF16),
        scratch_shapes=[
            pltpu.VMEM((nq, DIFF_V_DIM + ONES_ROWS, t), BF16),
            pltpu.VMEM((t, 128), F32),
            pltpu.VMEM((t, t), F32),
            pltpu.VMEM((t, 2 * t), F32),
            pltpu.VMEM((t, 2 * t), F32),
            stat, stat,
            stat, stat,
            stat,
            pltpu.VMEM((DIFF_V_DIM + ONES_ROWS, 2 * t), F32),
        ],
        compiler_params=_cparams(("parallel", "parallel", "arbitrary")),
        name="diff_attn",
    )(slopes, aqkv, aqkv, aqkv, row(lq1), row(lk1), row(lq2), row(lk2), row(subln_g))


def _swa_body(slopes_ref, sinks_ref, q_ref, kvp_ref, kvm_ref, on_ref, o_ref, bias_ref):
    tq = q_ref.shape[0]
    span = WINDOW + tq
    ti = pl.program_id(1)

    @pl.when((pl.program_id(0) == 0) & (ti == 0))
    def _():
        ii = lax.broadcasted_iota(jnp.int32, (tq, span), 0)
        jj = lax.broadcasted_iota(jnp.int32, (tq, span), 1)
        kc = _chunk_of(jj) - WINDOW // CHUNK
        qc = _chunk_of(ii)
        band = (kc <= qc) & (kc >= qc - WINDOW // CHUNK)
        dist = jnp.abs(ii + WINDOW - jj).astype(F32)
        for hh in range(SWA_HEADS):
            bias_ref[hh] = jnp.where(band, -slopes_ref[hh] * dist, NEG_INF)

    jmin = jnp.where(ti == 0, WINDOW, 0)
    in_seq = lax.broadcasted_iota(jnp.int32, (tq, span), 1) >= jmin

    outs = []
    for kvh in range(SWA_KV_HEADS):
        ks = slice(kvh * SWA_HEAD_DIM, (kvh + 1) * SWA_HEAD_DIM)
        vs = slice(SWA_KV_WIDTH + kvh * SWA_HEAD_DIM, SWA_KV_WIDTH + (kvh + 1) * SWA_HEAD_DIM)
        k = jnp.concatenate([kvp_ref[:, ks], kvm_ref[:, ks]], axis=0)
        v = jnp.concatenate([kvp_ref[:, vs], kvm_ref[:, vs]], axis=0)
        for g in range(SWA_GROUP):
            hh = kvh * SWA_GROUP + g
            qh = q_ref[:, hh * SWA_HEAD_DIM:(hh + 1) * SWA_HEAD_DIM]
            s = lax.dot_general(qh, k, (((1,), (1,)), ((), ())), preferred_element_type=F32)
            s = jnp.where(in_seq, s + bias_ref[hh], NEG_INF)
            sink = sinks_ref[hh]
            m = jnp.maximum(jnp.max(s, axis=1, keepdims=True), sink)
            p = jnp.exp(s - m)
            denom = jnp.sum(p, axis=1, keepdims=True) + jnp.exp(sink - m)
            outs.append(jnp.dot(p.astype(BF16), v, preferred_element_type=F32) / denom)
    o_ref[...] = _rms(jnp.concatenate(outs, axis=1), on_ref[...]).astype(BF16)


def _swa(cqkv, sinks, out_norm, batch, seq):
    n = cqkv.shape[0]
    tq = min(SWA_TQ, seq)
    nq = seq // tq
    wpt = tq // WINDOW
    kv_col = SWA_WIDTH // (2 * SWA_KV_WIDTH)
    slopes = 2.0 ** (-8.0 * jnp.arange(1, SWA_HEADS + 1, dtype=F32) / SWA_HEADS)
    return pl.pallas_call(
        _swa_body,
        grid=(batch, nq),
        in_specs=[
            pl.BlockSpec(memory_space=pltpu.SMEM),
            pl.BlockSpec(memory_space=pltpu.SMEM),
            pl.BlockSpec((tq, SWA_WIDTH), lambda b, i: (b * nq + i, 0)),
            pl.BlockSpec((WINDOW, 2 * SWA_KV_WIDTH),
                         lambda b, i: ((b * nq + i) * wpt - jnp.where(i == 0, 0, 1), kv_col)),
            pl.BlockSpec((tq, 2 * SWA_KV_WIDTH), lambda b, i: (b * nq + i, kv_col)),
            pl.BlockSpec((1, SWA_WIDTH), lambda b, i: (0, 0)),
        ],
        out_specs=pl.BlockSpec((tq, SWA_WIDTH), lambda b, i: (b * nq + i, 0)),
        out_shape=jax.ShapeDtypeStruct((n, SWA_WIDTH), BF16),
        scratch_shapes=[pltpu.VMEM((SWA_HEADS, tq, WINDOW + tq), F32)],
        compiler_params=_cparams(("arbitrary", "arbitrary")),
        name="swa",
    )(slopes, sinks.astype(F32), cqkv, cqkv, cqkv, out_norm.reshape(1, -1))


def _out_proj_body(x_ref, oa_ref, ob_ref, oc_ref, w_ref, o_ref):
    y = jnp.dot(oa_ref[...], w_ref[0:DIFF_WIDTH, :], preferred_element_type=F32)
    y = y + jnp.dot(ob_ref[...], w_ref[DIFF_WIDTH:DIFF_WIDTH + LRU_WIDTH, :], preferred_element_type=F32)
    y = y + jnp.dot(oc_ref[...], w_ref[DIFF_WIDTH + LRU_WIDTH:, :], preferred_element_type=F32)
    o_ref[...] = x_ref[...] + y


def _out_proj(x, oa, ob, oc, w_out):
    n, d = x.shape
    tm = min(MIX_TM, n)
    tok = lambda w: pl.BlockSpec((tm, w), lambda i: (i, 0))
    return pl.pallas_call(
        _out_proj_body,
        grid=(n // tm,),
        in_specs=[tok(d), tok(DIFF_WIDTH), tok(LRU_WIDTH), tok(SWA_WIDTH),
                  pl.BlockSpec(w_out.shape, lambda i: (0, 0))],
        out_specs=tok(d),
        out_shape=jax.ShapeDtypeStruct((n, d), F32),
        compiler_params=_cparams(("parallel",)),
        name="out_proj",
    )(x, oa, ob, oc, w_out)


def kernel(x, ffn1_norm, ffn1_w_gate, ffn1_w_up, ffn1_w_down, mix_norm, w_in, diff_lq1, diff_lk1, diff_lq2,
           diff_lk2, diff_subln, lru_conv_w, lru_conv_b, lru_w_rgate, lru_b_rgate, lru_w_igate, lru_b_igate,
           lru_lambda, lru_out_norm, swa_sinks, swa_out_norm, w_out, ffn2_norm, ffn2_w_gate, ffn2_w_up,
           ffn2_w_down, final_norm):
    batch, seq, d = x.shape
    depth = w_in.shape[0]
    h = x.reshape(batch * seq, d)
    bf = lambda w: w.astype(BF16)
    for l in range(depth):
        h = _ffn(h, ffn1_norm[l], bf(ffn1_w_gate[l]), bf(ffn1_w_up[l]), bf(ffn1_w_down[l]))
        aqkv, cqkv, ob = _mix_in(h, mix_norm[l], bf(w_in[l]), lru_conv_w[l], lru_conv_b[l],
                                 bf(lru_w_rgate[l]), lru_b_rgate[l], bf(lru_w_igate[l]), lru_b_igate[l],
                                 lru_lambda[l], lru_out_norm[l], seq)
        lam_init = 0.8 - 0.6 * math.exp(-0.3 * l)
        oa = _diff_attn(aqkv, diff_lq1[l], diff_lk1[l], diff_lq2[l], diff_lk2[l], diff_subln[l],
                        lam_init, batch, seq)
        oc = _swa(cqkv, swa_sinks[l], swa_out_norm[l], batch, seq)
        h = _out_proj(h, oa, ob, oc, bf(w_out[l]))
        h = _ffn(h, ffn2_norm[l], bf(ffn2_w_gate[l]), bf(ffn2_w_up[l]), bf(ffn2_w_down[l]),
                 final_g=final_norm if l == depth - 1 else None)
    return h.reshape(batch, seq, d)
```

```python
import functools
import math

import jax
import jax.numpy as jnp
from jax import lax
from jax.experimental import pallas as pl
from jax.experimental.pallas import tpu as pltpu

F32 = jnp.float32
BF16 = jnp.bfloat16

CHUNK = 64
DIFF_HEADS = 4
DIFF_HEAD_DIM = 64
DIFF_V_DIM = 128
DIFF_WIDTH = DIFF_HEADS * DIFF_V_DIM
LRU_WIDTH = 1024
LRU_BLOCKS = 8
LRU_BLOCK = 128
CONV_WIDTH = 4
LRU_C = 8.0
SWA_HEADS = 8
SWA_KV_HEADS = 2
SWA_HEAD_DIM = 64
SWA_GROUP = SWA_HEADS // SWA_KV_HEADS
SWA_WIDTH = SWA_HEADS * SWA_HEAD_DIM
SWA_KV_WIDTH = SWA_KV_HEADS * SWA_HEAD_DIM
WINDOW = 128
NORM_EPS = 1e-6
NEG_INF = -1e30
LOG2E = math.log2(math.e)
ONES_ROWS = 16

COL_AQ = 0
COL_BX = 3 * DIFF_WIDTH
COL_BG = COL_BX + LRU_WIDTH
COL_CQ = COL_BG + LRU_WIDTH
IN_COLS = COL_CQ + SWA_WIDTH + 2 * SWA_KV_WIDTH
A_COLS = 3 * DIFF_WIDTH
C_COLS = SWA_WIDTH + 2 * SWA_KV_WIDTH

VMEM_LIMIT_BYTES = 60 * 1024 * 1024

FFN_TM = 1024
FFN_TF = 512
FFN_SUB = 512
MIX_TM = 512
ATT_T = 512
ATT_UNROLL = 4
SWA_TQ = 256


def _rms(x, g):
    return x * lax.rsqrt(jnp.mean(x * x, axis=-1, keepdims=True) + NORM_EPS) * g


def _chunk_of(pos):
    return lax.shift_right_logical(pos, jnp.int32(CHUNK.bit_length() - 1))


def _cparams(sem):
    return pltpu.CompilerParams(dimension_semantics=sem, vmem_limit_bytes=VMEM_LIMIT_BYTES)


def _ffn_body(*refs, n_f, final):
    if final:
        x_ref, g_ref, wg_ref, wu_ref, wd_ref, fg_ref, o_ref, hn_ref = refs
    else:
        x_ref, g_ref, wg_ref, wu_ref, wd_ref, o_ref, hn_ref = refs
    f = pl.program_id(1)
    tm = x_ref.shape[0]

    @pl.when(f == 0)
    def _():
        for r in range(tm // FFN_SUB):
            rows = pl.ds(r * FFN_SUB, FFN_SUB)
            x = x_ref[rows, :]
            hn_ref[rows, :] = _rms(x, g_ref[...]).astype(BF16)
            o_ref[rows, :] = x

    for r in range(tm // FFN_SUB):
        rows = pl.ds(r * FFN_SUB, FFN_SUB)
        h = hn_ref[rows, :]
        g = jnp.dot(h, wg_ref[...], preferred_element_type=F32)
        u = jnp.dot(h, wu_ref[...], preferred_element_type=F32)
        mid = (0.5 * g * jax.nn.sigmoid(g) * u).astype(BF16)
        o_ref[rows, :] += jnp.dot(mid, wd_ref[...], preferred_element_type=F32)

    if final:
        @pl.when(f == n_f - 1)
        def _():
            for r in range(tm // FFN_SUB):
                rows = pl.ds(r * FFN_SUB, FFN_SUB)
                o_ref[rows, :] = _rms(o_ref[rows, :], fg_ref[...])


def _ffn(x, norm_g, w_gate, w_up, w_down, final_g=None):
    n, d = x.shape
    d_ff = w_gate.shape[1]
    tm = min(FFN_TM, n)
    n_f = d_ff // FFN_TF
    final = final_g is not None
    in_specs = [
        pl.BlockSpec((tm, d), lambda i, f: (i, 0)),
        pl.BlockSpec((1, d), lambda i, f: (0, 0)),
        pl.BlockSpec((d, FFN_TF), lambda i, f: (0, f)),
        pl.BlockSpec((d, FFN_TF), lambda i, f: (0, f)),
        pl.BlockSpec((FFN_TF, d), lambda i, f: (f, 0)),
    ]
    args = [x, norm_g.reshape(1, d), w_gate, w_up, w_down]
    if final:
        in_specs.append(pl.BlockSpec((1, d), lambda i, f: (0, 0)))
        args.append(final_g.reshape(1, d))
    return pl.pallas_call(
        functools.partial(_ffn_body, n_f=n_f, final=final),
        grid=(n // tm, n_f),
        in_specs=in_specs,
        out_specs=pl.BlockSpec((tm, d), lambda i, f: (i, 0)),
        out_shape=jax.ShapeDtypeStruct((n, d), F32),
        scratch_shapes=[pltpu.VMEM((tm, d), BF16)],
        compiler_params=_cparams(("parallel", "arbitrary")),
        name="ffn_final" if final else "ffn",
    )(*args)


def _mix_in_body(x_ref, g_ref, w_ref, cw_ref, cb_ref, wr_ref, br_ref, wi_ref, bi_ref, lam_ref, on_ref,
                 aqkv_ref, cqkv_ref, ob_ref, hn_ref, bx_ref, xc_ref, a_ref, u_ref, gate_ref, hc_ref, *, tiles_per_seq):
    tm = x_ref.shape[0]
    pad = 8

    @pl.when(pl.program_id(0) % tiles_per_seq == 0)
    def _():
        bx_ref[0:pad, :] = jnp.zeros((pad, LRU_WIDTH), F32)
        hc_ref[...] = jnp.zeros_like(hc_ref)

    hn_ref[...] = _rms(x_ref[...], g_ref[...]).astype(BF16)

    def proj(c0, width):
        return jnp.dot(hn_ref[...], w_ref[:, c0:c0 + width], preferred_element_type=F32)

    for c in range(LRU_WIDTH // 512):
        bx_ref[pad:pad + tm, c * 512:(c + 1) * 512] = proj(COL_BX + c * 512, 512)
    xc = cb_ref[...] + cw_ref[0:1, :] * bx_ref[pad - 3:pad - 3 + tm, :]
    for j in range(1, CONV_WIDTH):
        xc = xc + cw_ref[j:j + 1, :] * bx_ref[pad - 3 + j:pad - 3 + j + tm, :]
    xc_ref[...] = xc
    bx_ref[pad - 3:pad, :] = bx_ref[pad - 3 + tm:pad + tm, :]

    neg_c_sp = -LRU_C * jax.nn.softplus(-lam_ref[...])
    for nb in range(LRU_BLOCKS):
        sl = slice(nb * LRU_BLOCK, (nb + 1) * LRU_BLOCK)
        xcs = xc_ref[:, sl]
        xb = xcs.astype(BF16)
        r = jax.nn.sigmoid(jnp.dot(xb, wr_ref[nb], preferred_element_type=F32) + br_ref[:, sl])
        ig = jax.nn.sigmoid(jnp.dot(xb, wi_ref[nb], preferred_element_type=F32) + bi_ref[:, sl])
        log_a = r * neg_c_sp[:, sl]
        a_ref[:, sl] = jnp.exp(log_a)
        th = jnp.tanh(log_a)
        u_ref[:, sl] = jnp.sqrt(-2.0 * th / (1.0 - th)) * (ig * xcs)

    h = hc_ref[...]
    for t in range(tm):
        h = a_ref[t:t + 1, :] * h + u_ref[t:t + 1, :]
        u_ref[t:t + 1, :] = h
    hc_ref[...] = h

    aqkv_ref[:, 0:DIFF_WIDTH] = (proj(COL_AQ, DIFF_WIDTH) * (LOG2E * DIFF_HEAD_DIM ** -0.5)).astype(BF16)
    aqkv_ref[:, DIFF_WIDTH:2 * DIFF_WIDTH] = proj(COL_AQ + DIFF_WIDTH, DIFF_WIDTH).astype(BF16)
    aqkv_ref[:, 2 * DIFF_WIDTH:A_COLS] = proj(COL_AQ + 2 * DIFF_WIDTH, DIFF_WIDTH).astype(BF16)
    cqkv_ref[:, 0:SWA_WIDTH] = (proj(COL_CQ, SWA_WIDTH) * (SWA_HEAD_DIM ** -0.5)).astype(BF16)
    cqkv_ref[:, SWA_WIDTH:C_COLS] = proj(COL_CQ + SWA_WIDTH, 2 * SWA_KV_WIDTH).astype(BF16)
    for c in range(LRU_WIDTH // 512):
        gate_ref[:, c * 512:(c + 1) * 512] = jax.nn.gelu(proj(COL_BG + c * 512, 512))

    ob_ref[...] = _rms(gate_ref[...] * u_ref[...], on_ref[...]).astype(BF16)


def _mix_in(x, norm_g, w_in, conv_w, conv_b, w_r, b_r, w_i, b_i, lam, out_norm, seq):
    n, d = x.shape
    tm = min(MIX_TM, seq)
    row = lambda v: v.reshape(1, -1)
    const2 = lambda i: (0, 0)
    const3 = lambda i: (0, 0, 0)
    wvec = pl.BlockSpec((1, LRU_WIDTH), const2)
    gate_w = pl.BlockSpec((LRU_BLOCKS, LRU_BLOCK, LRU_BLOCK), const3)
    return pl.pallas_call(
        functools.partial(_mix_in_body, tiles_per_seq=seq // tm),
        grid=(n // tm,),
        in_specs=[
            pl.BlockSpec((tm, d), lambda i: (i, 0)),
            pl.BlockSpec((1, d), const2),
            pl.BlockSpec((d, IN_COLS), const2, pipeline_mode=pl.Buffered(1)),
            pl.BlockSpec((CONV_WIDTH, LRU_WIDTH), const2),
            wvec, gate_w, wvec, gate_w, wvec, wvec, wvec,
        ],
        out_specs=[
            pl.BlockSpec((tm, A_COLS), lambda i: (i, 0)),
            pl.BlockSpec((tm, C_COLS), lambda i: (i, 0)),
            pl.BlockSpec((tm, LRU_WIDTH), lambda i: (i, 0)),
        ],
        out_shape=[
            jax.ShapeDtypeStruct((n, A_COLS), BF16),
            jax.ShapeDtypeStruct((n, C_COLS), BF16),
            jax.ShapeDtypeStruct((n, LRU_WIDTH), BF16),
        ],
        scratch_shapes=[
            pltpu.VMEM((tm, d), BF16),
            pltpu.VMEM((tm + 8, LRU_WIDTH), F32),
            pltpu.VMEM((tm, LRU_WIDTH), F32),
            pltpu.VMEM((tm, LRU_WIDTH), F32),
            pltpu.VMEM((tm, LRU_WIDTH), F32),
            pltpu.VMEM((tm, LRU_WIDTH), F32),
            pltpu.VMEM((1, LRU_WIDTH), F32),
        ],
        compiler_params=_cparams(("arbitrary",)),
        name="mix_in",
    )(x, row(norm_g), w_in, conv_w, row(conv_b), w_r, row(b_r), w_i, row(b_i), row(lam), row(out_norm))


def _diff_body(slopes_ref, q_ref, k_ref, v_ref, lq1_ref, lk1_ref, lq2_ref, lk2_ref, sg_ref, o_ref,
               vt_ref, kp_ref, db_ref, z0_ref, z1_ref, mm0_ref, mm1_ref, al0_ref, al1_ref, m_ref, acc_ref,
               *, lam_init):
    t = q_ref.shape[0]
    r = 2 * t
    dv = v_ref.shape[1]
    h = pl.program_id(1)
    qi = pl.program_id(2)
    slope = slopes_ref[h] * LOG2E
    z_refs, mm_refs, al_refs = (z0_ref, z1_ref), (mm0_ref, mm1_ref), (al0_ref, al1_ref)

    @pl.when(qi == 0)
    def _():
        def xpose(c, carry):
            rows = pl.ds(pl.multiple_of(c * t, t), t)
            vt_ref[c, 0:dv, :] = v_ref[rows, :].T
            vt_ref[c, dv:dv + ONES_ROWS, :] = jnp.ones((ONES_ROWS, t), BF16)
            return carry

        lax.fori_loop(0, k_ref.shape[0] // t, xpose, 0)
        jj = lax.broadcasted_iota(jnp.int32, (t, t), 0)
        ii = lax.broadcasted_iota(jnp.int32, (t, t), 1)
        db_ref[...] = jnp.where(_chunk_of(jj) <= _chunk_of(ii), slope * (ii - jnp.abs(ii - jj)).astype(F32), NEG_INF)
        rowj = lax.broadcasted_iota(jnp.int32, kp_ref.shape, 0)
        lanej = lax.broadcasted_iota(jnp.int32, kp_ref.shape, 1)
        hi = lax.shift_right_logical(rowj, jnp.int32(7)).astype(F32)
        lo = jnp.bitwise_and(rowj, 127).astype(F32)
        kp_ref[...] = jnp.where(lanej < 2, hi, jnp.where(lanej < 4, lo, 0.0)).astype(BF16)

    q = q_ref[...]
    lane = lax.broadcasted_iota(jnp.int32, q.shape, 1)
    zero = jnp.zeros_like(q)
    qq = jnp.concatenate([jnp.where(lane < DIFF_HEAD_DIM, q, zero),
                          jnp.where(lane >= DIFF_HEAD_DIM, q, zero)], axis=0)
    sv = jnp.full((r, 128), slope, F32)
    s_hi = sv.astype(BF16).astype(F32)
    s_lo = (sv - s_hi).astype(BF16).astype(F32)
    lane2 = lax.broadcasted_iota(jnp.int32, (r, 128), 1)
    sl = jnp.where(lane2 == 0, s_hi * 128.0, jnp.where(lane2 == 1, s_lo * 128.0,
                   jnp.where(lane2 == 2, s_hi, jnp.where(lane2 == 3, s_lo, 0.0))))
    qq = jnp.concatenate([qq, sl.astype(BF16)], axis=1)

    m_ref[...] = jnp.full_like(m_ref, NEG_INF)
    acc_ref[...] = jnp.zeros_like(acc_ref)

    def pass1(kt, slot, diag):
        rows = pl.ds(pl.multiple_of(kt * t, t), t)
        pos = jnp.zeros(kp_ref.shape, BF16) if diag else kp_ref[...]
        kk = jnp.concatenate([k_ref[rows, :], pos], axis=1)
        s = lax.dot_general(kk, qq, (((1,), (1,)), ((), ())), preferred_element_type=F32)
        if diag:
            z = s + jnp.concatenate([db_ref[...]] * 2, axis=1)
            shift = 0.0
        else:
            z = s
            shift = slope * ((kt - qi) * t).astype(F32)
        z_refs[slot][...] = z
        m_prev = m_ref[...]
        m_new = jnp.maximum(m_prev, jnp.max(z, axis=0, keepdims=True) + shift)
        m_ref[...] = m_new
        mm_refs[slot][...] = m_new - shift
        al_refs[slot][...] = jnp.exp2(m_prev - m_new)

    def pass2(tile, slot):
        p = jnp.exp2(z_refs[slot][...] - mm_refs[slot][...]).astype(BF16)
        acc_ref[...] = al_refs[slot][...] * acc_ref[...] + jnp.dot(vt_ref[tile], p, preferred_element_type=F32)

    def step(j, parity):
        pass1(j, 1 - parity, False)
        pass2(jnp.where(j == 0, qi, j - 1), parity)

    pass1(qi, 0, True)

    def unrolled(jj, carry):
        for u in range(ATT_UNROLL):
            step(ATT_UNROLL * jj + u, u % 2)
        return carry

    lax.fori_loop(0, qi // ATT_UNROLL, unrolled, 0)
    done = (qi // ATT_UNROLL) * ATT_UNROLL
    for u in range(ATT_UNROLL - 1):
        @pl.when(qi - done > u)
        def _():
            step(done + u, u % 2)

    for parity in range(2):
        @pl.when(qi % 2 == parity)
        def _():
            pass2(jnp.where(qi == 0, qi, qi - 1), parity)

    lam = (jnp.exp(jnp.sum(lq1_ref[...] * lk1_ref[...], axis=-1, keepdims=True))
           - jnp.exp(jnp.sum(lq2_ref[...] * lk2_ref[...], axis=-1, keepdims=True)) + lam_init)
    inv = 1.0 / acc_ref[dv:dv + 1, :]
    o_t = acc_ref[0:dv, 0:t] * inv[:, 0:t] - lam * (acc_ref[0:dv, t:r] * inv[:, t:r])
    o_ref[...] = (_rms(o_t.T, sg_ref[...]) * (1.0 - lam_init)).astype(BF16)


def _diff_attn(aqkv, lq1, lk1, lq2, lk2, subln_g, lam_init, batch, seq):
    n = aqkv.shape[0]
    t = min(ATT_T, seq)
    assert t & (t - 1) == 0 and seq % t == 0 and t % CHUNK == 0
    nq = seq // t
    slopes = 2.0 ** (-8.0 * jnp.arange(1, DIFF_HEADS + 1, dtype=F32) / DIFF_HEADS)
    row = lambda v: v.reshape(1, -1)
    vec = lambda w: pl.BlockSpec((1, w), lambda b, h, i: (0, 0))
    stat = pltpu.VMEM((1, 2 * t), F32)
    return pl.pallas_call(
        functools.partial(_diff_body, lam_init=lam_init),
        grid=(batch, DIFF_HEADS, nq),
        in_specs=[
            pl.BlockSpec(memory_space=pltpu.SMEM),
            pl.BlockSpec((t, DIFF_V_DIM), lambda b, h, i: (b * nq + i, h)),
            pl.BlockSpec((seq, DIFF_V_DIM), lambda b, h, i: (b, DIFF_HEADS + h)),
            pl.BlockSpec((seq, DIFF_V_DIM), lambda b, h, i: (b, 2 * DIFF_HEADS + h)),
            vec(DIFF_HEAD_DIM), vec(DIFF_HEAD_DIM), vec(DIFF_HEAD_DIM), vec(DIFF_HEAD_DIM),
            vec(DIFF_V_DIM),
        ],
        out_specs=pl.BlockSpec((t, DIFF_V_DIM), lambda b, h, i: (b * nq + i, h)),
        out_shape=jax.ShapeDtypeStruct((n, DIFF_WIDTH), BF16),
        scratch_shapes=[
            pltpu.VMEM((nq, DIFF_V_DIM + ONES_ROWS, t), BF16),
            pltpu.VMEM((t, 128), BF16),
            pltpu.VMEM((t, t), F32),
            pltpu.VMEM((t, 2 * t), F32),
            pltpu.VMEM((t, 2 * t), F32),
            stat, stat,
            stat, stat,
            stat,
            pltpu.VMEM((DIFF_V_DIM + ONES_ROWS, 2 * t), F32),
        ],
        compiler_params=_cparams(("parallel", "parallel", "arbitrary")),
        name="diff_attn",
    )(slopes, aqkv, aqkv, aqkv, row(lq1), row(lk1), row(lq2), row(lk2), row(subln_g))


def _swa_body(slopes_ref, sinks_ref, q_ref, kvp_ref, kvm_ref, on_ref, o_ref, bias_ref):
    tq = q_ref.shape[0]
    span = WINDOW + tq
    ti = pl.program_id(1)

    @pl.when((pl.program_id(0) == 0) & (ti == 0))
    def _():
        ii = lax.broadcasted_iota(jnp.int32, (tq, span), 0)
        jj = lax.broadcasted_iota(jnp.int32, (tq, span), 1)
        kc = _chunk_of(jj) - WINDOW // CHUNK
        qc = _chunk_of(ii)
        band = (kc <= qc) & (kc >= qc - WINDOW // CHUNK)
        dist = jnp.abs(ii + WINDOW - jj).astype(F32)
        for hh in range(SWA_HEADS):
            bias_ref[hh] = jnp.where(band, -slopes_ref[hh] * dist, NEG_INF)

    jmin = jnp.where(ti == 0, WINDOW, 0)
    in_seq = lax.broadcasted_iota(jnp.int32, (tq, span), 1) >= jmin

    outs = []
    for kvh in range(SWA_KV_HEADS):
        ks = slice(kvh * SWA_HEAD_DIM, (kvh + 1) * SWA_HEAD_DIM)
        vs = slice(SWA_KV_WIDTH + kvh * SWA_HEAD_DIM, SWA_KV_WIDTH + (kvh + 1) * SWA_HEAD_DIM)
        k = jnp.concatenate([kvp_ref[:, ks], kvm_ref[:, ks]], axis=0)
        v = jnp.concatenate([kvp_ref[:, vs], kvm_ref[:, vs]], axis=0)
        for g in range(SWA_GROUP):
            hh = kvh * SWA_GROUP + g
            qh = q_ref[:, hh * SWA_HEAD_DIM:(hh + 1) * SWA_HEAD_DIM]
            s = lax.dot_general(qh, k, (((1,), (1,)), ((), ())), preferred_element_type=F32)
            s = jnp.where(in_seq, s + bias_ref[hh], NEG_INF)
            sink = sinks_ref[hh]
            m = jnp.maximum(jnp.max(s, axis=1, keepdims=True), sink)
            p = jnp.exp(s - m)
            denom = jnp.sum(p, axis=1, keepdims=True) + jnp.exp(sink - m)
            outs.append(jnp.dot(p.astype(BF16), v, preferred_element_type=F32) / denom)
    o_ref[...] = _rms(jnp.concatenate(outs, axis=1), on_ref[...]).astype(BF16)


def _swa(cqkv, sinks, out_norm, batch, seq):
    n = cqkv.shape[0]
    tq = min(SWA_TQ, seq)
    nq = seq // tq
    wpt = tq // WINDOW
    kv_col = SWA_WIDTH // (2 * SWA_KV_WIDTH)
    slopes = 2.0 ** (-8.0 * jnp.arange(1, SWA_HEADS + 1, dtype=F32) / SWA_HEADS)
    return pl.pallas_call(
        _swa_body,
        grid=(batch, nq),
        in_specs=[
            pl.BlockSpec(memory_space=pltpu.SMEM),
            pl.BlockSpec(memory_space=pltpu.SMEM),
            pl.BlockSpec((tq, SWA_WIDTH), lambda b, i: (b * nq + i, 0)),
            pl.BlockSpec((WINDOW, 2 * SWA_KV_WIDTH),
                         lambda b, i: ((b * nq + i) * wpt - jnp.where(i == 0, 0, 1), kv_col)),
            pl.BlockSpec((tq, 2 * SWA_KV_WIDTH), lambda b, i: (b * nq + i, kv_col)),
            pl.BlockSpec((1, SWA_WIDTH), lambda b, i: (0, 0)),
        ],
        out_specs=pl.BlockSpec((tq, SWA_WIDTH), lambda b, i: (b * nq + i, 0)),
        out_shape=jax.ShapeDtypeStruct((n, SWA_WIDTH), BF16),
        scratch_shapes=[pltpu.VMEM((SWA_HEADS, tq, WINDOW + tq), F32)],
        compiler_params=_cparams(("arbitrary", "arbitrary")),
        name="swa",
    )(slopes, sinks.astype(F32), cqkv, cqkv, cqkv, out_norm.reshape(1, -1))


def _out_proj_body(x_ref, oa_ref, ob_ref, oc_ref, w_ref, o_ref):
    y = jnp.dot(oa_ref[...], w_ref[0:DIFF_WIDTH, :], preferred_element_type=F32)
    y = y + jnp.dot(ob_ref[...], w_ref[DIFF_WIDTH:DIFF_WIDTH + LRU_WIDTH, :], preferred_element_type=F32)
    y = y + jnp.dot(oc_ref[...], w_ref[DIFF_WIDTH + LRU_WIDTH:, :], preferred_element_type=F32)
    o_ref[...] = x_ref[...] + y


def _out_proj(x, oa, ob, oc, w_out):
    n, d = x.shape
    tm = min(MIX_TM, n)
    tok = lambda w: pl.BlockSpec((tm, w), lambda i: (i, 0))
    return pl.pallas_call(
        _out_proj_body,
        grid=(n // tm,),
        in_specs=[tok(d), tok(DIFF_WIDTH), tok(LRU_WIDTH), tok(SWA_WIDTH),
                  pl.BlockSpec(w_out.shape, lambda i: (0, 0))],
        out_specs=tok(d),
        out_shape=jax.ShapeDtypeStruct((n, d), F32),
        compiler_params=_cparams(("parallel",)),
        name="out_proj",
    )(x, oa, ob, oc, w_out)


def kernel(x, ffn1_norm, ffn1_w_gate, ffn1_w_up, ffn1_w_down, mix_norm, w_in, diff_lq1, diff_lk1, diff_lq2,
           diff_lk2, diff_subln, lru_conv_w, lru_conv_b, lru_w_rgate, lru_b_rgate, lru_w_igate, lru_b_igate,
           lru_lambda, lru_out_norm, swa_sinks, swa_out_norm, w_out, ffn2_norm, ffn2_w_gate, ffn2_w_up,
           ffn2_w_down, final_norm):
    batch, seq, d = x.shape
    depth = w_in.shape[0]
    h = x.reshape(batch * seq, d)
    bf = lambda w: w.astype(BF16)
    for l in range(depth):
        h = _ffn(h, ffn1_norm[l], bf(ffn1_w_gate[l]), bf(ffn1_w_up[l]), bf(ffn1_w_down[l]))
        aqkv, cqkv, ob = _mix_in(h, mix_norm[l], bf(w_in[l]), lru_conv_w[l], lru_conv_b[l],
                                 bf(lru_w_rgate[l]), lru_b_rgate[l], bf(lru_w_igate[l]), lru_b_igate[l],
                                 lru_lambda[l], lru_out_norm[l], seq)
        lam_init = 0.8 - 0.6 * math.exp(-0.3 * l)
        oa = _diff_attn(aqkv, diff_lq1[l], diff_lk1[l], diff_lq2[l], diff_lk2[l], diff_subln[l],
                        lam_init, batch, seq)
        oc = _swa(cqkv, swa_sinks[l], swa_out_norm[l], batch, seq)
        h = _out_proj(h, oa, ob, oc, bf(w_out[l]))
        h = _ffn(h, ffn2_norm[l], bf(ffn2_w_gate[l]), bf(ffn2_w_up[l]), bf(ffn2_w_down[l]),
                 final_g=final_norm if l == depth - 1 else None)
    return h.reshape(batch, seq, d)
```

```python
import functools
import math

import jax
import jax.numpy as jnp
from jax import lax
from jax.experimental import pallas as pl
from jax.experimental.pallas import tpu as pltpu

F32 = jnp.float32
BF16 = jnp.bfloat16

CHUNK = 64
DIFF_HEADS = 4
DIFF_HEAD_DIM = 64
DIFF_V_DIM = 128
DIFF_WIDTH = DIFF_HEADS * DIFF_V_DIM
LRU_WIDTH = 1024
LRU_BLOCKS = 8
LRU_BLOCK = 128
CONV_WIDTH = 4
LRU_C = 8.0
SWA_HEADS = 8
SWA_KV_HEADS = 2
SWA_HEAD_DIM = 64
SWA_GROUP = SWA_HEADS // SWA_KV_HEADS
SWA_WIDTH = SWA_HEADS * SWA_HEAD_DIM
SWA_KV_WIDTH = SWA_KV_HEADS * SWA_HEAD_DIM
WINDOW = 128
NORM_EPS = 1e-6
NEG_INF = -1e30
LOG2E = math.log2(math.e)
ONES_ROWS = 16

COL_AQ = 0
COL_BX = 3 * DIFF_WIDTH
COL_BG = COL_BX + LRU_WIDTH
COL_CQ = COL_BG + LRU_WIDTH
IN_COLS = COL_CQ + SWA_WIDTH + 2 * SWA_KV_WIDTH
A_COLS = 3 * DIFF_WIDTH
C_COLS = SWA_WIDTH + 2 * SWA_KV_WIDTH

VMEM_LIMIT_BYTES = 60 * 1024 * 1024

FFN_TM = 1024
FFN_TF = 512
FFN_SUB = 512
MIX_TM = 512
MIX_SUB = 256
ATT_T = 512
ATT_UNROLL = 4
SWA_TQ = 256


def _rms(x, g):
    return x * lax.rsqrt(jnp.mean(x * x, axis=-1, keepdims=True) + NORM_EPS) * g


def _chunk_of(pos):
    return lax.shift_right_logical(pos, jnp.int32(CHUNK.bit_length() - 1))


def _cparams(sem):
    return pltpu.CompilerParams(dimension_semantics=sem, vmem_limit_bytes=VMEM_LIMIT_BYTES)


def _ffn_body(*refs, n_f, final):
    if final:
        x_ref, g_ref, wg_ref, wu_ref, wd_ref, fg_ref, o_ref, hn_ref = refs
    else:
        x_ref, g_ref, wg_ref, wu_ref, wd_ref, o_ref, hn_ref = refs
    f = pl.program_id(1)
    tm = x_ref.shape[0]

    @pl.when(f == 0)
    def _():
        for r in range(tm // FFN_SUB):
            rows = pl.ds(r * FFN_SUB, FFN_SUB)
            x = x_ref[rows, :]
            hn_ref[rows, :] = _rms(x, g_ref[...]).astype(BF16)
            o_ref[rows, :] = x

    for r in range(tm // FFN_SUB):
        rows = pl.ds(r * FFN_SUB, FFN_SUB)
        h = hn_ref[rows, :]
        g = jnp.dot(h, wg_ref[...], preferred_element_type=F32)
        u = jnp.dot(h, wu_ref[...], preferred_element_type=F32)
        mid = (0.5 * g * jax.nn.sigmoid(g) * u).astype(BF16)
        o_ref[rows, :] += jnp.dot(mid, wd_ref[...], preferred_element_type=F32)

    if final:
        @pl.when(f == n_f - 1)
        def _():
            for r in range(tm // FFN_SUB):
                rows = pl.ds(r * FFN_SUB, FFN_SUB)
                o_ref[rows, :] = _rms(o_ref[rows, :], fg_ref[...])


def _ffn(x, norm_g, w_gate, w_up, w_down, final_g=None):
    n, d = x.shape
    d_ff = w_gate.shape[1]
    tm = min(FFN_TM, n)
    n_f = d_ff // FFN_TF
    final = final_g is not None
    in_specs = [
        pl.BlockSpec((tm, d), lambda i, f: (i, 0)),
        pl.BlockSpec((1, d), lambda i, f: (0, 0)),
        pl.BlockSpec((d, FFN_TF), lambda i, f: (0, f)),
        pl.BlockSpec((d, FFN_TF), lambda i, f: (0, f)),
        pl.BlockSpec((FFN_TF, d), lambda i, f: (f, 0)),
    ]
    args = [x, norm_g.reshape(1, d), w_gate, w_up, w_down]
    if final:
        in_specs.append(pl.BlockSpec((1, d), lambda i, f: (0, 0)))
        args.append(final_g.reshape(1, d))
    return pl.pallas_call(
        functools.partial(_ffn_body, n_f=n_f, final=final),
        grid=(n // tm, n_f),
        in_specs=in_specs,
        out_specs=pl.BlockSpec((tm, d), lambda i, f: (i, 0)),
        out_shape=jax.ShapeDtypeStruct((n, d), F32),
        scratch_shapes=[pltpu.VMEM((tm, d), BF16)],
        compiler_params=_cparams(("parallel", "arbitrary")),
        name="ffn_final" if final else "ffn",
    )(*args)


def _mix_in_body(x_ref, g_ref, w_ref, cw_ref, cb_ref, wri_ref, br_ref, bi_ref, lam_ref, on_ref,
                 aqkv_ref, cqkv_ref, ob_ref, hn_ref, bx_ref, xc_ref, a_ref, u_ref, gate_ref, hc_ref, *, tiles_per_seq):
    tm = x_ref.shape[0]
    pad = 8

    @pl.when(pl.program_id(0) % tiles_per_seq == 0)
    def _():
        bx_ref[0:pad, :] = jnp.zeros((pad, LRU_WIDTH), F32)
        hc_ref[...] = jnp.zeros_like(hc_ref)

    hn_ref[...] = _rms(x_ref[...], g_ref[...]).astype(BF16)

    def proj(c0, width):
        w = w_ref[:, c0:c0 + width]
        return jnp.concatenate([jnp.dot(hn_ref[r0:r0 + MIX_SUB, :], w, preferred_element_type=F32)
                                for r0 in range(0, tm, MIX_SUB)], axis=0)

    for c in range(LRU_WIDTH // 512):
        bx_ref[pad:pad + tm, c * 512:(c + 1) * 512] = proj(COL_BX + c * 512, 512)
    xc = cb_ref[...] + cw_ref[0:1, :] * bx_ref[pad - 3:pad - 3 + tm, :]
    for j in range(1, CONV_WIDTH):
        xc = xc + cw_ref[j:j + 1, :] * bx_ref[pad - 3 + j:pad - 3 + j + tm, :]
    xc_ref[...] = xc
    bx_ref[pad - 3:pad, :] = bx_ref[pad - 3 + tm:pad + tm, :]

    neg_c_sp = -LRU_C * jax.nn.softplus(-lam_ref[...])
    for nb in range(LRU_BLOCKS):
        sl = slice(nb * LRU_BLOCK, (nb + 1) * LRU_BLOCK)
        xcs = xc_ref[:, sl]
        xb = xcs.astype(BF16)
        ri = jnp.dot(xb, wri_ref[nb], preferred_element_type=F32)
        r = jax.nn.sigmoid(ri[:, 0:LRU_BLOCK] + br_ref[:, sl])
        ig = jax.nn.sigmoid(ri[:, LRU_BLOCK:2 * LRU_BLOCK] + bi_ref[:, sl])
        log_a = r * neg_c_sp[:, sl]
        a_ref[:, sl] = jnp.exp(log_a)
        th = jnp.tanh(log_a)
        u_ref[:, sl] = jnp.sqrt(-2.0 * th / (1.0 - th)) * (ig * xcs)

    h = hc_ref[...]
    for t in range(tm):
        h = a_ref[t:t + 1, :] * h + u_ref[t:t + 1, :]
        u_ref[t:t + 1, :] = h
    hc_ref[...] = h

    aqkv_ref[:, 0:DIFF_WIDTH] = (proj(COL_AQ, DIFF_WIDTH) * (LOG2E * DIFF_HEAD_DIM ** -0.5)).astype(BF16)
    aqkv_ref[:, DIFF_WIDTH:2 * DIFF_WIDTH] = proj(COL_AQ + DIFF_WIDTH, DIFF_WIDTH).astype(BF16)
    aqkv_ref[:, 2 * DIFF_WIDTH:A_COLS] = proj(COL_AQ + 2 * DIFF_WIDTH, DIFF_WIDTH).astype(BF16)
    cqkv_ref[:, 0:SWA_WIDTH] = (proj(COL_CQ, SWA_WIDTH) * (SWA_HEAD_DIM ** -0.5)).astype(BF16)
    cqkv_ref[:, SWA_WIDTH:C_COLS] = proj(COL_CQ + SWA_WIDTH, 2 * SWA_KV_WIDTH).astype(BF16)
    for c in range(LRU_WIDTH // 512):
        gate_ref[:, c * 512:(c + 1) * 512] = jax.nn.gelu(proj(COL_BG + c * 512, 512))

    ob_ref[...] = _rms(gate_ref[...] * u_ref[...], on_ref[...]).astype(BF16)


def _mix_in(x, norm_g, w_in, conv_w, conv_b, w_r, b_r, w_i, b_i, lam, out_norm, seq):
    n, d = x.shape
    tm = min(MIX_TM, seq)
    row = lambda v: v.reshape(1, -1)
    const2 = lambda i: (0, 0)
    const3 = lambda i: (0, 0, 0)
    wvec = pl.BlockSpec((1, LRU_WIDTH), const2)
    gate_w = pl.BlockSpec((LRU_BLOCKS, LRU_BLOCK, 2 * LRU_BLOCK), const3)
    w_ri = jnp.concatenate([w_r, w_i], axis=-1)
    return pl.pallas_call(
        functools.partial(_mix_in_body, tiles_per_seq=seq // tm),
        grid=(n // tm,),
        in_specs=[
            pl.BlockSpec((tm, d), lambda i: (i, 0)),
            pl.BlockSpec((1, d), const2),
            pl.BlockSpec((d, IN_COLS), const2, pipeline_mode=pl.Buffered(1)),
            pl.BlockSpec((CONV_WIDTH, LRU_WIDTH), const2),
            wvec, gate_w, wvec, wvec, wvec, wvec,
        ],
        out_specs=[
            pl.BlockSpec((tm, A_COLS), lambda i: (i, 0)),
            pl.BlockSpec((tm, C_COLS), lambda i: (i, 0)),
            pl.BlockSpec((tm, LRU_WIDTH), lambda i: (i, 0)),
        ],
        out_shape=[
            jax.ShapeDtypeStruct((n, A_COLS), BF16),
            jax.ShapeDtypeStruct((n, C_COLS), BF16),
            jax.ShapeDtypeStruct((n, LRU_WIDTH), BF16),
        ],
        scratch_shapes=[
            pltpu.VMEM((tm, d), BF16),
            pltpu.VMEM((tm + 8, LRU_WIDTH), F32),
            pltpu.VMEM((tm, LRU_WIDTH), F32),
            pltpu.VMEM((tm, LRU_WIDTH), F32),
            pltpu.VMEM((tm, LRU_WIDTH), F32),
            pltpu.VMEM((tm, LRU_WIDTH), F32),
            pltpu.VMEM((1, LRU_WIDTH), F32),
        ],
        compiler_params=_cparams(("arbitrary",)),
        name="mix_in",
    )(x, row(norm_g), w_in, conv_w, row(conv_b), w_ri, row(b_r), row(b_i), row(lam), row(out_norm))


def _diff_body(slopes_ref, q_ref, k_ref, v_ref, lq1_ref, lk1_ref, lq2_ref, lk2_ref, sg_ref, o_ref,
               vt_ref, kp_ref, db_ref, z0_ref, z1_ref, mm0_ref, mm1_ref, al0_ref, al1_ref, m_ref, acc_ref,
               *, lam_init):
    t = q_ref.shape[0]
    r = 2 * t
    dv = v_ref.shape[1]
    h = pl.program_id(1)
    qi = pl.program_id(2)
    slope = slopes_ref[h] * LOG2E
    z_refs, mm_refs, al_refs = (z0_ref, z1_ref), (mm0_ref, mm1_ref), (al0_ref, al1_ref)

    @pl.when(qi == 0)
    def _():
        def xpose(c, carry):
            rows = pl.ds(pl.multiple_of(c * t, t), t)
            vt_ref[c, 0:dv, :] = v_ref[rows, :].T
            vt_ref[c, dv:dv + ONES_ROWS, :] = jnp.ones((ONES_ROWS, t), BF16)
            return carry

        lax.fori_loop(0, k_ref.shape[0] // t, xpose, 0)
        jj = lax.broadcasted_iota(jnp.int32, (t, t), 0)
        ii = lax.broadcasted_iota(jnp.int32, (t, t), 1)
        db_ref[...] = jnp.where(_chunk_of(jj) <= _chunk_of(ii), slope * (ii - jnp.abs(ii - jj)).astype(F32), NEG_INF)
        rowj = lax.broadcasted_iota(jnp.int32, kp_ref.shape, 0)
        lanej = lax.broadcasted_iota(jnp.int32, kp_ref.shape, 1)
        hi = lax.shift_right_logical(rowj, jnp.int32(7)).astype(F32)
        lo = jnp.bitwise_and(rowj, 127).astype(F32)
        kp_ref[...] = jnp.where(lanej < 2, hi, jnp.where(lanej < 4, lo, 0.0)).astype(BF16)

    q = q_ref[...]
    lane = lax.broadcasted_iota(jnp.int32, q.shape, 1)
    zero = jnp.zeros_like(q)
    qq = jnp.concatenate([jnp.where(lane < DIFF_HEAD_DIM, q, zero),
                          jnp.where(lane >= DIFF_HEAD_DIM, q, zero)], axis=0)
    sv = jnp.full((r, 128), slope, F32)
    s_hi = sv.astype(BF16).astype(F32)
    s_lo = (sv - s_hi).astype(BF16).astype(F32)
    lane2 = lax.broadcasted_iota(jnp.int32, (r, 128), 1)
    sl = jnp.where(lane2 == 0, s_hi * 128.0, jnp.where(lane2 == 1, s_lo * 128.0,
                   jnp.where(lane2 == 2, s_hi, jnp.where(lane2 == 3, s_lo, 0.0))))
    qq = jnp.concatenate([qq, sl.astype(BF16)], axis=1)

    m_ref[...] = jnp.full_like(m_ref, NEG_INF)
    acc_ref[...] = jnp.zeros_like(acc_ref)

    def pass1(kt, slot, diag):
        rows = pl.ds(pl.multiple_of(kt * t, t), t)
        pos = jnp.zeros(kp_ref.shape, BF16) if diag else kp_ref[...]
        kk = jnp.concatenate([k_ref[rows, :], pos], axis=1)
        s = lax.dot_general(kk, qq, (((1,), (1,)), ((), ())), preferred_element_type=F32)
        if diag:
            z = s + jnp.concatenate([db_ref[...]] * 2, axis=1)
            shift = 0.0
        else:
            z = s
            shift = slope * ((kt - qi) * t).astype(F32)
        z_refs[slot][...] = z
        m_prev = m_ref[...]
        m_new = jnp.maximum(m_prev, jnp.max(z, axis=0, keepdims=True) + shift)
        m_ref[...] = m_new
        mm_refs[slot][...] = m_new - shift
        al_refs[slot][...] = jnp.exp2(m_prev - m_new)

    def pass2(tile, slot):
        p = jnp.exp2(z_refs[slot][...] - mm_refs[slot][...]).astype(BF16)
        acc_ref[...] = al_refs[slot][...] * acc_ref[...] + jnp.dot(vt_ref[tile], p, preferred_element_type=F32)

    def step(j, parity):
        pass1(j, 1 - parity, False)
        pass2(jnp.where(j == 0, qi, j - 1), parity)

    pass1(qi, 0, True)

    def unrolled(jj, carry):
        for u in range(ATT_UNROLL):
            step(ATT_UNROLL * jj + u, u % 2)
        return carry

    lax.fori_loop(0, qi // ATT_UNROLL, unrolled, 0)
    done = (qi // ATT_UNROLL) * ATT_UNROLL
    for u in range(ATT_UNROLL - 1):
        @pl.when(qi - done > u)
        def _():
            step(done + u, u % 2)

    for parity in range(2):
        @pl.when(qi % 2 == parity)
        def _():
            pass2(jnp.where(qi == 0, qi, qi - 1), parity)

    lam = (jnp.exp(jnp.sum(lq1_ref[...] * lk1_ref[...], axis=-1, keepdims=True))
           - jnp.exp(jnp.sum(lq2_ref[...] * lk2_ref[...], axis=-1, keepdims=True)) + lam_init)
    inv = 1.0 / acc_ref[dv:dv + 1, :]
    o_t = acc_ref[0:dv, 0:t] * inv[:, 0:t] - lam * (acc_ref[0:dv, t:r] * inv[:, t:r])
    o_ref[...] = (_rms(o_t.T, sg_ref[...]) * (1.0 - lam_init)).astype(BF16)


def _diff_attn(aqkv, lq1, lk1, lq2, lk2, subln_g, lam_init, batch, seq):
    n = aqkv.shape[0]
    t = min(ATT_T, seq)
    assert t & (t - 1) == 0 and seq % t == 0 and t % CHUNK == 0
    nq = seq // t
    slopes = 2.0 ** (-8.0 * jnp.arange(1, DIFF_HEADS + 1, dtype=F32) / DIFF_HEADS)
    row = lambda v: v.reshape(1, -1)
    vec = lambda w: pl.BlockSpec((1, w), lambda b, h, i: (0, 0))
    stat = pltpu.VMEM((1, 2 * t), F32)
    return pl.pallas_call(
        functools.partial(_diff_body, lam_init=lam_init),
        grid=(batch, DIFF_HEADS, nq),
        in_specs=[
            pl.BlockSpec(memory_space=pltpu.SMEM),
            pl.BlockSpec((t, DIFF_V_DIM), lambda b, h, i: (b * nq + i, h)),
            pl.BlockSpec((seq, DIFF_V_DIM), lambda b, h, i: (b, DIFF_HEADS + h)),
            pl.BlockSpec((seq, DIFF_V_DIM), lambda b, h, i: (b, 2 * DIFF_HEADS + h)),
            vec(DIFF_HEAD_DIM), vec(DIFF_HEAD_DIM), vec(DIFF_HEAD_DIM), vec(DIFF_HEAD_DIM),
            vec(DIFF_V_DIM),
        ],
        out_specs=pl.BlockSpec((t, DIFF_V_DIM), lambda b, h, i: (b * nq + i, h)),
        out_shape=jax.ShapeDtypeStruct((n, DIFF_WIDTH), BF16),
        scratch_shapes=[
            pltpu.VMEM((nq, DIFF_V_DIM + ONES_ROWS, t), BF16),
            pltpu.VMEM((t, 128), BF16),
            pltpu.VMEM((t, t), F32),
            pltpu.VMEM((t, 2 * t), F32),
            pltpu.VMEM((t, 2 * t), F32),
            stat, stat,
            stat, stat,
            stat,
            pltpu.VMEM((DIFF_V_DIM + ONES_ROWS, 2 * t), F32),
        ],
        compiler_params=_cparams(("parallel", "parallel", "arbitrary")),
        name="diff_attn",
    )(slopes, aqkv, aqkv, aqkv, row(lq1), row(lk1), row(lq2), row(lk2), row(subln_g))


def _swa_body(slopes_ref, sinks_ref, q_ref, kvp_ref, kvm_ref, on_ref, o_ref, bias_ref):
    tq = q_ref.shape[0]
    span = WINDOW + tq
    ti = pl.program_id(1)

    @pl.when((pl.program_id(0) == 0) & (ti == 0))
    def _():
        ii = lax.broadcasted_iota(jnp.int32, (tq, span), 0)
        jj = lax.broadcasted_iota(jnp.int32, (tq, span), 1)
        kc = _chunk_of(jj) - WINDOW // CHUNK
        qc = _chunk_of(ii)
        band = (kc <= qc) & (kc >= qc - WINDOW // CHUNK)
        dist = jnp.abs(ii + WINDOW - jj).astype(F32)
        for hh in range(SWA_HEADS):
            bias_ref[hh] = jnp.where(band, -slopes_ref[hh] * dist, NEG_INF)

    jmin = jnp.where(ti == 0, WINDOW, 0)
    in_seq = lax.broadcasted_iota(jnp.int32, (tq, span), 1) >= jmin

    outs = []
    for kvh in range(SWA_KV_HEADS):
        ks = slice(kvh * SWA_HEAD_DIM, (kvh + 1) * SWA_HEAD_DIM)
        vs = slice(SWA_KV_WIDTH + kvh * SWA_HEAD_DIM, SWA_KV_WIDTH + (kvh + 1) * SWA_HEAD_DIM)
        k = jnp.concatenate([kvp_ref[:, ks], kvm_ref[:, ks]], axis=0)
        v = jnp.concatenate([kvp_ref[:, vs], kvm_ref[:, vs]], axis=0)
        for g in range(SWA_GROUP):
            hh = kvh * SWA_GROUP + g
            qh = q_ref[:, hh * SWA_HEAD_DIM:(hh + 1) * SWA_HEAD_DIM]
            s = lax.dot_general(qh, k, (((1,), (1,)), ((), ())), preferred_element_type=F32)
            s = jnp.where(in_seq, s + bias_ref[hh], NEG_INF)
            sink = sinks_ref[hh]
            m = jnp.maximum(jnp.max(s, axis=1, keepdims=True), sink)
            p = jnp.exp(s - m)
            denom = jnp.sum(p, axis=1, keepdims=True) + jnp.exp(sink - m)
            outs.append(jnp.dot(p.astype(BF16), v, preferred_element_type=F32) / denom)
    o_ref[...] = _rms(jnp.concatenate(outs, axis=1), on_ref[...]).astype(BF16)


def _swa(cqkv, sinks, out_norm, batch, seq):
    n = cqkv.shape[0]
    tq = min(SWA_TQ, seq)
    nq = seq // tq
    wpt = tq // WINDOW
    kv_col = SWA_WIDTH // (2 * SWA_KV_WIDTH)
    slopes = 2.0 ** (-8.0 * jnp.arange(1, SWA_HEADS + 1, dtype=F32) / SWA_HEADS)
    return pl.pallas_call(
        _swa_body,
        grid=(batch, nq),
        in_specs=[
            pl.BlockSpec(memory_space=pltpu.SMEM),
            pl.BlockSpec(memory_space=pltpu.SMEM),
            pl.BlockSpec((tq, SWA_WIDTH), lambda b, i: (b * nq + i, 0)),
            pl.BlockSpec((WINDOW, 2 * SWA_KV_WIDTH),
                         lambda b, i: ((b * nq + i) * wpt - jnp.where(i == 0, 0, 1), kv_col)),
            pl.BlockSpec((tq, 2 * SWA_KV_WIDTH), lambda b, i: (b * nq + i, kv_col)),
            pl.BlockSpec((1, SWA_WIDTH), lambda b, i: (0, 0)),
        ],
        out_specs=pl.BlockSpec((tq, SWA_WIDTH), lambda b, i: (b * nq + i, 0)),
        out_shape=jax.ShapeDtypeStruct((n, SWA_WIDTH), BF16),
        scratch_shapes=[pltpu.VMEM((SWA_HEADS, tq, WINDOW + tq), F32)],
        compiler_params=_cparams(("arbitrary", "arbitrary")),
        name="swa",
    )(slopes, sinks.astype(F32), cqkv, cqkv, cqkv, out_norm.reshape(1, -1))


def _out_proj_body(x_ref, oa_ref, ob_ref, oc_ref, w_ref, o_ref):
    y = jnp.dot(oa_ref[...], w_ref[0:DIFF_WIDTH, :], preferred_element_type=F32)
    y = y + jnp.dot(ob_ref[...], w_ref[DIFF_WIDTH:DIFF_WIDTH + LRU_WIDTH, :], preferred_element_type=F32)
    y = y + jnp.dot(oc_ref[...], w_ref[DIFF_WIDTH + LRU_WIDTH:, :], preferred_element_type=F32)
    o_ref[...] = x_ref[...] + y


def _out_proj(x, oa, ob, oc, w_out):
    n, d = x.shape
    tm = min(MIX_TM, n)
    tok = lambda w: pl.BlockSpec((tm, w), lambda i: (i, 0))
    return pl.pallas_call(
        _out_proj_body,
        grid=(n // tm,),
        in_specs=[tok(d), tok(DIFF_WIDTH), tok(LRU_WIDTH), tok(SWA_WIDTH),
                  pl.BlockSpec(w_out.shape, lambda i: (0, 0))],
        out_specs=tok(d),
        out_shape=jax.ShapeDtypeStruct((n, d), F32),
        compiler_params=_cparams(("parallel",)),
        name="out_proj",
    )(x, oa, ob, oc, w_out)


def kernel(x, ffn1_norm, ffn1_w_gate, ffn1_w_up, ffn1_w_down, mix_norm, w_in, diff_lq1, diff_lk1, diff_lq2,
           diff_lk2, diff_subln, lru_conv_w, lru_conv_b, lru_w_rgate, lru_b_rgate, lru_w_igate, lru_b_igate,
           lru_lambda, lru_out_norm, swa_sinks, swa_out_norm, w_out, ffn2_norm, ffn2_w_gate, ffn2_w_up,
           ffn2_w_down, final_norm):
    batch, seq, d = x.shape
    depth = w_in.shape[0]
    h = x.reshape(batch * seq, d)
    bf = lambda w: w.astype(BF16)
    for l in range(depth):
        h = _ffn(h, ffn1_norm[l], bf(ffn1_w_gate[l]), bf(ffn1_w_up[l]), bf(ffn1_w_down[l]))
        aqkv, cqkv, ob = _mix_in(h, mix_norm[l], bf(w_in[l]), lru_conv_w[l], lru_conv_b[l],
                                 bf(lru_w_rgate[l]), lru_b_rgate[l], bf(lru_w_igate[l]), lru_b_igate[l],
                                 lru_lambda[l], lru_out_norm[l], seq)
        lam_init = 0.8 - 0.6 * math.exp(-0.3 * l)
        oa = _diff_attn(aqkv, diff_lq1[l], diff_lk1[l], diff_lq2[l], diff_lk2[l], diff_subln[l],
                        lam_init, batch, seq)
        oc = _swa(cqkv, swa_sinks[l], swa_out_norm[l], batch, seq)
        h = _out_proj(h, oa, ob, oc, bf(w_out[l]))
        h = _ffn(h, ffn2_norm[l], bf(ffn2_w_gate[l]), bf(ffn2_w_up[l]), bf(ffn2_w_down[l]),
                 final_g=final_norm if l == depth - 1 else None)
    return h.reshape(batch, seq, d)
```
